```python
import math
import jax, jax.numpy as jnp
from jax import lax
import numpy as np

D_MODEL = 1024
BATCH = 32
SEQ = 2048
DEPTH = 2
DEC_BATCH = 32
DEC_SEQ = 64
PAST_LEN = 2048

CHUNK = 64
D_MIX = D_MODEL
RET_HEADS = 4
RET_DIM = 128
D_RET = RET_HEADS * RET_DIM
D_LRU = D_MIX // 4
LRU_BLOCKS = 4
LRU_BLOCK = D_LRU // LRU_BLOCKS
LRU_CONV = 4
LRU_C = 8.0
D_S5 = D_MIX - D_RET - D_LRU
S5_GROUP = 16
S5_GROUPS = D_S5 // S5_GROUP
S5_STATE = 64
D_FF = 2816
FFN_CONV = 3
ROPE_BASE = 10000.0
EPS = 1e-6
D_IN = 4 * D_RET + 2 * D_LRU + D_S5
IN_SPLITS = (D_RET, 2 * D_RET, 3 * D_RET, 4 * D_RET, 4 * D_RET + D_LRU, 4 * D_RET + 2 * D_LRU)

kernel_name = 'hymba_retention_rglru_s5_convffn_step'


def rmsnorm(x, w):
    xf = x.astype(jnp.float32)
    y = xf * lax.rsqrt(jnp.mean(xf * xf, axis=-1, keepdims=True) + EPS)
    return (y * w.astype(jnp.float32)).astype(x.dtype)


def rotary(x, pos):
    half = x.shape[-1] // 2
    freq = ROPE_BASE ** (-jnp.arange(half, dtype=jnp.float32) / half)
    ang = pos.astype(jnp.float32)[:, None] * freq[None, :]
    cos = jnp.cos(ang)[None, :, None, :]
    sin = jnp.sin(ang)[None, :, None, :]
    x1, x2 = x[..., :half], x[..., half:]
    return jnp.concatenate([x1 * cos - x2 * sin, x1 * sin + x2 * cos], axis=-1)


def ret_log_decay():
    return jnp.log1p(-(2.0 ** (-5.0 - jnp.arange(RET_HEADS, dtype=jnp.float32))))


def causal_dwconv(x, buf, w, b):
    K, C = w.shape
    T = x.shape[1]
    xp = jnp.concatenate([buf.astype(x.dtype), x], axis=1)
    y = lax.conv_general_dilated(xp, w[:, None, :].astype(x.dtype), window_strides=(1,), padding='VALID',
                                 dimension_numbers=('NWC', 'WIO', 'NWC'), feature_group_count=C)
    return y + b.astype(x.dtype), xp[:, T:]


def retention_block(q, k, v, S, log_g):
    L = q.shape[2]
    n = jnp.arange(L, dtype=jnp.float32)
    diff = n[:, None] - n[None, :]
    decay = jnp.where(diff >= 0, jnp.exp(log_g[:, None, None] * jnp.maximum(diff, 0.0)), 0.0)
    scores = jnp.einsum('bhld,bhmd->bhlm', q, k) * decay
    o_inner = jnp.einsum('bhlm,bhme->bhle', scores, v)
    o_cross = jnp.einsum('bhld,bhde->bhle', q, S) * jnp.exp(log_g[:, None] * (n + 1.0))[..., None]
    k_dec = k * jnp.exp(log_g[:, None] * (L - 1.0 - n))[..., None]
    S_new = jnp.exp(log_g * L)[:, None, None] * S + jnp.einsum('bhld,bhle->bhde', k_dec, v)
    return o_inner + o_cross, S_new


def retention_seq(q, k, v, S0, log_g):
    B, T, H, d = q.shape
    L = min(T, CHUNK)
    n_blk = T // L

    def blocks(a):
        return a.reshape(B, n_blk, L, H, a.shape[-1]).transpose(1, 0, 3, 2, 4)

    def step(S, qkv):
        qb, kb, vb = qkv
        o, S = retention_block(qb, kb, vb, S, log_g)
        return S, o

    S_fin, o = lax.scan(step, S0, (blocks(q), blocks(k), blocks(v)))
    o = o.transpose(1, 0, 3, 2, 4).reshape(B, T, H, -1)
    return o, S_fin


def linear_comb(left, right):
    a1, b1 = left
    a2, b2 = right
    return a1 * a2, a2 * b1 + b2


def rg_lru(x, h0, gate_w, gate_b, lam):
    B, T, _ = x.shape
    xb = x.reshape(B, T, LRU_BLOCKS, LRU_BLOCK)
    gates = jnp.einsum('btnc,gncd->gbtnd', xb, gate_w).reshape(2, B, T, D_LRU) + gate_b[:, None, None, :]
    r = jax.nn.sigmoid(gates[0])
    i = jax.nn.sigmoid(gates[1])
    log_a = -LRU_C * r * jax.nn.softplus(-lam)
    a = jnp.exp(log_a)
    b = jnp.sqrt(-jnp.expm1(2.0 * log_a)) * (i * x)
    b = b.at[:, 0].add(a[:, 0] * h0)
    _, h = lax.associative_scan(linear_comb, (a, b), axis=1)
    return h, h[:, -1]


def s5_comb(left, right):
    ar1, ai1, br1, bi1 = left
    ar2, ai2, br2, bi2 = right
    return (ar2 * ar1 - ai2 * ai1, ar2 * ai1 + ai2 * ar1,
            ar2 * br1 - ai2 * bi1 + br2, ar2 * bi1 + ai2 * br1 + bi2)


def s5_scan(u, x0, lam_re, lam_im, log_dt, Bm, Cm, Dv):
    Bsz, T, _ = u.shape
    dt = jnp.exp(log_dt)[:, None]
    mag = jnp.exp(lam_re * dt)
    ph = lam_im * dt
    ab_re, ab_im = mag * jnp.cos(ph), mag * jnp.sin(ph)
    den = lam_re * lam_re + lam_im * lam_im
    num_re = ab_re - 1.0
    f_re = (num_re * lam_re + ab_im * lam_im) / den
    f_im = (ab_im * lam_re - num_re * lam_im) / den
    B_re, B_im = Bm[..., 0], Bm[..., 1]
    bb_re = f_re[..., None] * B_re - f_im[..., None] * B_im
    bb_im = f_re[..., None] * B_im + f_im[..., None] * B_re
    ug = u.reshape(Bsz, T, S5_GROUPS, S5_GROUP)
    bu_re = jnp.einsum('btgh,gph->tbgp', ug, bb_re)
    bu_im = jnp.einsum('btgh,gph->tbgp', ug, bb_im)
    x0_re, x0_im = x0[..., 0], x0[..., 1]
    bu_re = bu_re.at[0].add(ab_re * x0_re - ab_im * x0_im)
    bu_im = bu_im.at[0].add(ab_re * x0_im + ab_im * x0_re)
    a_re = jnp.broadcast_to(ab_re, (T, 1, S5_GROUPS, S5_STATE))
    a_im = jnp.broadcast_to(ab_im, (T, 1, S5_GROUPS, S5_STATE))
    _, _, xr, xi = lax.associative_scan(s5_comb, (a_re, a_im, bu_re, bu_im), axis=0)
    y = jnp.einsum('tbgp,ghp->btgh', xr, Cm[..., 0]) - jnp.einsum('tbgp,ghp->btgh', xi, Cm[..., 1])
    y = y.reshape(Bsz, T, D_S5) + Dv * u
    return y, jnp.stack([xr[-1], xi[-1]], axis=-1)


def hybrid_layer(x, pos, states, params):
    ret_S, lru_h, lru_buf, s5_x, ffn_buf = states
    (norm1_w, w_in, mix_norm_w, lru_conv_w, lru_conv_b, lru_gate_w, lru_gate_b, lru_lambda,
     s5_lambda_re, s5_lambda_im, s5_log_dt, s5_B, s5_C, s5_D, s5_glu_w, s5_glu_b,
     w_out, norm2_w, ffn_up, ffn_conv_w, ffn_conv_b, ffn_down) = params
    f32 = jnp.float32
    B, T, _ = x.shape
    proj = (rmsnorm(x, norm1_w) @ w_in).astype(f32)
    q, k, v, g, lru_x, lru_gate, s5_u = jnp.split(proj, IN_SPLITS, axis=-1)
    mnw = mix_norm_w.astype(f32)
    q = rotary(q.reshape(B, T, RET_HEADS, RET_DIM), pos)
    k = rotary(k.reshape(B, T, RET_HEADS, RET_DIM), pos) * (RET_DIM ** -0.5)
    v = v.reshape(B, T, RET_HEADS, RET_DIM)
    o, ret_S_new = retention_seq(q, k, v, ret_S.astype(f32), ret_log_decay())
    o = o * lax.rsqrt(jnp.mean(o * o, axis=-1, keepdims=True) + EPS)
    ret_out = o.reshape(B, T, D_RET) * mnw[:D_RET] * jax.nn.silu(g)
    conv_x, lru_buf_new = causal_dwconv(lru_x, lru_buf, lru_conv_w.astype(f32), lru_conv_b.astype(f32))
    h, lru_h_new = rg_lru(conv_x, lru_h.astype(f32), lru_gate_w.astype(f32), lru_gate_b.astype(f32),
                          lru_lambda.astype(f32))
    lru_out = rmsnorm(jax.nn.gelu(lru_gate) * h, mnw[D_RET:D_RET + D_LRU])
    y, s5_x_new = s5_scan(s5_u, s5_x.astype(f32), s5_lambda_re.astype(f32), s5_lambda_im.astype(f32),
                          s5_log_dt.astype(f32), s5_B.astype(f32), s5_C.astype(f32), s5_D.astype(f32))
    z = jax.nn.gelu(y)
    s5_out = rmsnorm(z * jax.nn.sigmoid(z @ s5_glu_w.astype(f32) + s5_glu_b.astype(f32)),
                     mnw[D_RET + D_LRU:])
    mix = jnp.concatenate([ret_out, lru_out, s5_out], axis=-1).astype(x.dtype)
    x = x + mix @ w_out
    up = rmsnorm(x, norm2_w) @ ffn_up
    up, ffn_buf_new = causal_dwconv(up, ffn_buf, ffn_conv_w, ffn_conv_b)
    gate, val = jnp.split(up, 2, axis=-1)
    x = x + (jax.nn.silu(gate) * val) @ ffn_down
    return x, (ret_S_new, lru_h_new, lru_buf_new, s5_x_new, ffn_buf_new)


def run_trunk(x, pos, states, layer_params, final_norm_w):
    per_layer = []
    for layer in range(DEPTH):
        x, st = hybrid_layer(x, pos, tuple(s[layer] for s in states), tuple(p[layer] for p in layer_params))
        per_layer.append(st)
    stacked = tuple(jnp.stack([st[i] for st in per_layer]) for i in range(5))
    return rmsnorm(x, final_norm_w), stacked


def setup_inputs(seed: int = 0) -> dict:
    key = jax.random.key(seed)
    ks = jax.random.split(key, 40)
    f32 = jnp.float32
    nrm = lambda k, shape, s: s * jax.random.normal(k, shape, f32)
    lru_u = jax.random.uniform(ks[14], (DEPTH, D_LRU), f32, 0.9, 0.999) ** (1.0 / LRU_C)
    return {
        'x_prompt': nrm(ks[0], (BATCH, SEQ, D_MODEL), 1.0),
        'x_sample': nrm(ks[1], (DEC_BATCH, DEC_SEQ, D_MODEL), 1.0),
        'state_ret': nrm(ks[2], (DEPTH, DEC_BATCH, RET_HEADS, RET_DIM, RET_DIM), 0.5),
        'state_lru_h': nrm(ks[3], (DEPTH, DEC_BATCH, D_LRU), 0.5),
        'state_lru_conv': nrm(ks[4], (DEPTH, DEC_BATCH, LRU_CONV - 1, D_LRU), 1.0),
        'state_s5': nrm(ks[5], (DEPTH, DEC_BATCH, S5_GROUPS, S5_STATE, 2), 0.1),
        'state_ffn_conv': nrm(ks[6], (DEPTH, DEC_BATCH, FFN_CONV - 1, 2 * D_FF), 1.0),
        'norm1_w': 1.0 + nrm(ks[7], (DEPTH, D_MODEL), 0.02),
        'w_in': nrm(ks[8], (DEPTH, D_MODEL, D_IN), D_MODEL ** -0.5),
        'mix_norm_w': 1.0 + nrm(ks[9], (DEPTH, D_MIX), 0.02),
        'lru_conv_w': nrm(ks[10], (DEPTH, LRU_CONV, D_LRU), LRU_CONV ** -0.5),
        'lru_conv_b': nrm(ks[11], (DEPTH, D_LRU), 0.02),
        'lru_gate_w': nrm(ks[12], (DEPTH, 2, LRU_BLOCKS, LRU_BLOCK, LRU_BLOCK), LRU_BLOCK ** -0.5),
        'lru_gate_b': nrm(ks[13], (DEPTH, 2, D_LRU), 0.02),
        'lru_lambda': jnp.log(lru_u) - jnp.log1p(-lru_u),
        's5_lambda_re': -0.5 + nrm(ks[15], (DEPTH, S5_GROUPS, S5_STATE), 0.01),
        's5_lambda_im': math.pi * jnp.arange(S5_STATE, dtype=f32)[None, None, :]
                        + nrm(ks[16], (DEPTH, S5_GROUPS, S5_STATE), 0.01),
        's5_log_dt': jax.random.uniform(ks[17], (DEPTH, S5_GROUPS), f32, math.log(0.001), math.log(0.1)),
        's5_B': nrm(ks[18], (DEPTH, S5_GROUPS, S5_STATE, S5_GROUP, 2), (2.0 * S5_GROUP) ** -0.5),
        's5_C': nrm(ks[19], (DEPTH, S5_GROUPS, S5_GROUP, S5_STATE, 2), S5_STATE ** -0.5),
        's5_D': nrm(ks[20], (DEPTH, D_S5), 1.0),
        's5_glu_w': nrm(ks[21], (DEPTH, D_S5, D_S5), D_S5 ** -0.5),
        's5_glu_b': nrm(ks[22], (DEPTH, D_S5), 0.02),
        'w_out': nrm(ks[23], (DEPTH, D_MIX, D_MODEL), D_MIX ** -0.5),
        'norm2_w': 1.0 + nrm(ks[24], (DEPTH, D_MODEL), 0.02),
        'ffn_up': nrm(ks[25], (DEPTH, D_MODEL, 2 * D_FF), D_MODEL ** -0.5),
        'ffn_conv_w': nrm(ks[26], (DEPTH, FFN_CONV, 2 * D_FF), FFN_CONV ** -0.5),
        'ffn_conv_b': nrm(ks[27], (DEPTH, 2 * D_FF), 0.02),
        'ffn_down': nrm(ks[28], (DEPTH, D_FF, D_MODEL), D_FF ** -0.5),
        'final_norm_w': 1.0 + nrm(ks[29], (D_MODEL,), 0.02),
    }


def reference(x_prompt, x_sample, state_ret, state_lru_h, state_lru_conv, state_s5, state_ffn_conv,
              norm1_w, w_in, mix_norm_w, lru_conv_w, lru_conv_b, lru_gate_w, lru_gate_b, lru_lambda,
              s5_lambda_re, s5_lambda_im, s5_log_dt, s5_B, s5_C, s5_D, s5_glu_w, s5_glu_b,
              w_out, norm2_w, ffn_up, ffn_conv_w, ffn_conv_b, ffn_down, final_norm_w):
    f32 = jnp.float32
    layer_params = (norm1_w, w_in, mix_norm_w, lru_conv_w, lru_conv_b, lru_gate_w, lru_gate_b, lru_lambda,
                    s5_lambda_re, s5_lambda_im, s5_log_dt, s5_B, s5_C, s5_D, s5_glu_w, s5_glu_b,
                    w_out, norm2_w, ffn_up, ffn_conv_w, ffn_conv_b, ffn_down)
    Bp, Tp, _ = x_prompt.shape
    zero_states = (jnp.zeros((DEPTH, Bp, RET_HEADS, RET_DIM, RET_DIM), f32),
                   jnp.zeros((DEPTH, Bp, D_LRU), f32),
                   jnp.zeros((DEPTH, Bp, LRU_CONV - 1, D_LRU), f32),
                   jnp.zeros((DEPTH, Bp, S5_GROUPS, S5_STATE, 2), f32),
                   jnp.zeros((DEPTH, Bp, FFN_CONV - 1, 2 * D_FF), x_prompt.dtype))
    pos_prompt = jnp.arange(Tp, dtype=jnp.int32)
    y_prompt, new_p = run_trunk(x_prompt, pos_prompt, zero_states, layer_params, final_norm_w)
    ret_p, lru_h_p, lru_conv_p, s5_p, ffn_conv_p = new_p
    Ts = x_sample.shape[1]
    pos_sample = PAST_LEN + jnp.arange(Ts, dtype=jnp.int32)
    sample_states = (state_ret, state_lru_h, state_lru_conv, state_s5, state_ffn_conv)
    y_sample, new_s = run_trunk(x_sample, pos_sample, sample_states, layer_params, final_norm_w)
    ret_s, lru_h_s, lru_conv_s, s5_s, ffn_conv_s = new_s
    return (y_prompt, y_sample, ret_p, lru_h_p, lru_conv_p, s5_p, ffn_conv_p,
            ret_s, lru_h_s, lru_conv_s, s5_s, ffn_conv_s)
```

```python
import functools
import math

import numpy as np
import jax
import jax.numpy as jnp
from jax import lax
from jax.experimental import pallas as pl
from jax.experimental.pallas import tpu as pltpu

D_MODEL = 1024
DEPTH = 2
PAST_LEN = 2048
RET_HEADS = 4
RET_DIM = 128
D_RET = RET_HEADS * RET_DIM
D_LRU = 256
LRU_BLOCKS = 4
LRU_BLOCK = D_LRU // LRU_BLOCKS
LRU_CONV = 4
LRU_C = 8.0
D_S5 = 256
S5_GROUP = 16
S5_GROUPS = D_S5 // S5_GROUP
S5_STATE = 64
S5_N = S5_GROUPS * S5_STATE
D_FF = 2816
FFN_CONV = 3
ROPE_BASE = 10000.0
EPS = 1e-6
D_IN = 4 * D_RET + 2 * D_LRU + D_S5

G = 8
L = 64
R = L * G
LANES = 128
FF_TILE = D_FF // 2
VMEM_LIMIT = 56 * 1024 * 1024

F32 = jnp.float32
BF16 = jnp.bfloat16

_LOG_G = [math.log1p(-(2.0 ** (-5.0 - h))) for h in range(RET_HEADS)]


def _ret_tables():
    n = np.arange(L, dtype=np.float64)
    diff = n[:, None] - n[None, :]
    lg = np.asarray(_LOG_G)[:, None, None]
    dmask = np.where(diff >= 0, np.exp(lg * np.maximum(diff, 0.0)), 0.0)
    cdec = np.exp(lg[:, :, 0] * (n + 1.0))[:, :, None] * np.ones((1, 1, RET_DIM))
    kdec = np.exp(lg[:, :, 0] * (L - 1.0 - n))[:, :, None] * np.ones((1, 1, RET_DIM))
    return dmask.astype(np.float32), cdec.astype(np.float32), kdec.astype(np.float32)


_DMASK, _CDEC, _KDEC = _ret_tables()
_SDEC = [math.exp(lg * L) for lg in _LOG_G]


def _rms(x, w):
    return x * lax.rsqrt(jnp.mean(x * x, axis=-1, keepdims=True) + EPS) * w


def _dot(a, b):
    return jnp.dot(a, b, preferred_element_type=F32)


def _rope_kernel(cos_ref, sin_ref, *, pos0, rows):
    base = pl.program_id(0) * rows + pos0
    lane = lax.broadcasted_iota(jnp.int32, (rows, LANES), 1)
    row = lax.broadcasted_iota(jnp.int32, (rows, LANES), 0)
    half = RET_DIM // 2
    idx = jnp.where(lane >= half, lane - half, lane).astype(F32)
    freq = jnp.exp(idx * (-math.log(ROPE_BASE) / half))
    ang = (row + base).astype(F32) * freq
    s = jnp.sin(ang)
    cos_ref[...] = jnp.cos(ang)
    sin_ref[...] = jnp.where(lane >= half, s, -s)


def _rope_tables(T, pos0):
    rows = min(T, 256)
    return pl.pallas_call(
        functools.partial(_rope_kernel, pos0=pos0, rows=rows),
        grid=(T // rows,),
        out_specs=[pl.BlockSpec((rows, LANES), lambda i: (i, 0))] * 2,
        out_shape=[jax.ShapeDtypeStruct((T, LANES), F32)] * 2,
        name="rope_tables",
    )()


def _s5_prep_kernel(lre_ref, lim_ref, ldt_ref, bre_ref, bim_ref, ab_ref, bbre_ref, bbim_ref):
    dt = jnp.exp(ldt_ref[...])
    lre = lre_ref[...]
    lim = lim_ref[...]
    mag = jnp.exp(lre * dt)
    ph = lim * dt
    abr = mag * jnp.cos(ph)
    abi = mag * jnp.sin(ph)
    den = lre * lre + lim * lim
    nr = abr - 1.0
    fre = (nr * lre + abi * lim) / den
    fim = (abi * lre - nr * lim) / den
    ab_ref[0:1, :] = abr
    ab_ref[1:2, :] = abi
    bre = bre_ref[...]
    bim = bim_ref[...]
    bbre_ref[...] = fre * bre - fim * bim
    bbim_ref[...] = fre * bim + fim * bre


def _s5_discretize(lam_re, lam_im, log_dt, s5_B):
    lre = lam_re.reshape(DEPTH, 1, S5_N)
    lim = lam_im.reshape(DEPTH, 1, S5_N)
    ldt = jnp.broadcast_to(log_dt[:, :, None], (DEPTH, S5_GROUPS, S5_STATE)).reshape(DEPTH, 1, S5_N)
    bt = s5_B.transpose(0, 3, 1, 2, 4).reshape(DEPTH, S5_GROUP, S5_N, 2)
    row = lambda r: pl.BlockSpec((None, r, S5_N), lambda d: (d, 0, 0))
    return pl.pallas_call(
        _s5_prep_kernel,
        grid=(DEPTH,),
        in_specs=[row(1), row(1), row(1), row(S5_GROUP), row(S5_GROUP)],
        out_specs=[row(2), row(S5_GROUP), row(S5_GROUP)],
        out_shape=[jax.ShapeDtypeStruct((DEPTH, 2, S5_N), F32),
                   jax.ShapeDtypeStruct((DEPTH, S5_GROUP, S5_N), F32),
                   jax.ShapeDtypeStruct((DEPTH, S5_GROUP, S5_N), F32)],
        name="s5_discretize",
    )(lre, lim, ldt, bt[..., 0], bt[..., 1])


def _s5_in_matrix(bb):
    gi = jnp.arange(S5_GROUPS)
    same = gi[:, None, None, None] == gi[None, None, :, None]
    full = jnp.where(same, bb.reshape(1, S5_GROUP, S5_GROUPS, S5_STATE), 0.0)
    return full.reshape(D_S5, S5_N)


def _s5_out_matrix(c):
    gi = jnp.arange(S5_GROUPS)
    same = gi[:, None, None, None] == gi[None, None, :, None]
    full = jnp.where(same, c.transpose(2, 0, 1)[None], 0.0)
    return full.reshape(S5_N, D_S5)


def _lru_gate_matrix(gw):
    bi = jnp.arange(LRU_BLOCKS)
    same = bi[:, None, None, None, None] == bi[None, None, None, :, None]
    wt = gw.transpose(1, 2, 0, 3)[:, :, :, None, :]
    return jnp.where(same, wt, 0.0).reshape(D_LRU, 2 * D_LRU)


def _mixer_kernel(x_ref, cos_ref, sin_ref, sret_ref, slh_ref, slc_ref, ss5_ref,
                  dmask_ref, cdec_ref, kdec_ref,
                  n1w_ref, win_ref, mnw_ref, lcw_ref, lcb_ref, wg_ref, gb_ref, lam_ref,
                  ab_ref, bmat_ref, cre_ref, cim_ref, s5d_ref, gluw_ref, glub_ref, wout_ref,
                  xo_ref, ret_ref, lh_ref, lc_ref, s5s_ref,
                  h_scr, q_scr, k_scr, v_scr, o_scr, g_scr, ext_scr, lgate_scr, a_scr, b_scr,
                  u_scr, bu_scr, mix_scr):
    c = pl.program_id(1)
    n_carry = (LRU_CONV - 1) * G

    @pl.when(c == 0)
    def _():
        ret_ref[...] = sret_ref[...]
        lh_ref[...] = slh_ref[...]
        ext_scr[0:n_carry, :] = slc_ref[...]
        s5s_ref[...] = ss5_ref[...]

    h_scr[...] = _rms(x_ref[...], n1w_ref[...]).astype(BF16)
    hb = h_scr[...]
    for j, dst in enumerate((q_scr, k_scr, v_scr)):
        res = _dot(hb, win_ref[:, j * D_RET:(j + 1) * D_RET])
        for hh in range(RET_HEADS):
            dst[hh] = res[:, hh * RET_DIM:(hh + 1) * RET_DIM]
    g_scr[...] = _dot(hb, win_ref[:, 3 * D_RET:4 * D_RET])
    res = _dot(hb, win_ref[:, 4 * D_RET:4 * D_RET + 2 * D_LRU])
    ext_scr[n_carry:n_carry + R, :] = res[:, :D_LRU]
    lgate_scr[...] = res[:, D_LRU:]
    u_scr[...] = _dot(hb, win_ref[:, 4 * D_RET + 2 * D_LRU:])

    cosv = cos_ref[...]
    sinv = sin_ref[...]

    def ret_body(b, carry):
        rows = pl.ds(b, L, stride=G)
        for hh in range(RET_HEADS):
            qb = q_scr[hh, rows, :]
            kb = k_scr[hh, rows, :]
            vb = v_scr[hh, rows, :].astype(BF16)
            qr = qb * cosv + pltpu.roll(qb, RET_DIM // 2, 1) * sinv
            kr = (kb * cosv + pltpu.roll(kb, RET_DIM // 2, 1) * sinv) * (RET_DIM ** -0.5)
            qbf = qr.astype(BF16)
            s_old = ret_ref[b, hh]
            scores = lax.dot_general(qbf, kr.astype(BF16), (((1,), (1,)), ((), ())),
                                     preferred_element_type=F32) * dmask_ref[hh]
            o = _dot(scores.astype(BF16), vb) + _dot(qbf, s_old.astype(BF16)) * cdec_ref[hh]
            kd = (kr * kdec_ref[hh]).astype(BF16)
            ret_ref[b, hh] = _SDEC[hh] * s_old + lax.dot_general(
                kd, vb, (((0,), (0,)), ((), ())), preferred_element_type=F32)
            o_scr[hh, rows, :] = o * lax.rsqrt(jnp.mean(o * o, axis=-1, keepdims=True) + EPS)
        return carry

    lax.fori_loop(0, G, ret_body, 0)
    mnw = mnw_ref[...]
    for hh in range(RET_HEADS):
        cols = slice(hh * RET_DIM, (hh + 1) * RET_DIM)
        mix_scr[:, cols] = (o_scr[hh] * mnw[:, cols] * jax.nn.silu(g_scr[:, cols])).astype(BF16)

    conv = lcb_ref[...] + lcw_ref[0:1, :] * ext_scr[0:R, :]
    for j in range(1, LRU_CONV):
        conv = conv + lcw_ref[j:j + 1, :] * ext_scr[j * G:j * G + R, :]
    carry_rows = ext_scr[R:R + n_carry, :]
    ext_scr[0:n_carry, :] = carry_rows
    lc_ref[...] = carry_rows
    gates = _dot(conv.astype(BF16), wg_ref[...]) + gb_ref[...]
    z = -lam_ref[...]
    softplus = jnp.maximum(z, 0.0) + jnp.log1p(jnp.exp(-jnp.abs(z)))
    log_a = -LRU_C * jax.nn.sigmoid(gates[:, :D_LRU]) * softplus
    a = jnp.exp(log_a)
    a_scr[...] = a
    b_scr[...] = jnp.sqrt(-jnp.tanh(log_a) * (a * a + 1.0)) * (jax.nn.sigmoid(gates[:, D_LRU:]) * conv)

    def lru_body(t, h):
        rows = pl.ds(pl.multiple_of(t * G, G), G)
        h = a_scr[rows, :] * h + b_scr[rows, :]
        b_scr[rows, :] = h
        return h

    lh_ref[...] = lax.fori_loop(0, L, lru_body, lh_ref[...], unroll=8)
    lru_out = _rms(jax.nn.gelu(lgate_scr[...]) * b_scr[...], mnw[:, D_RET:D_RET + D_LRU])
    mix_scr[:, D_RET:D_RET + D_LRU] = lru_out.astype(BF16)

    u = u_scr[...]
    bu_scr[...] = _dot(u.astype(BF16), bmat_ref[...])
    ar = jnp.broadcast_to(ab_ref[0:1, :], (G, S5_N))
    ai = jnp.broadcast_to(ab_ref[1:2, :], (G, S5_N))

    def s5_body(t, carry):
        xr, xi = carry
        rows = pl.ds(pl.multiple_of(t * G, G), G)
        nr = ar * xr - ai * xi + bu_scr[rows, 0:S5_N]
        ni = ar * xi + ai * xr + bu_scr[rows, S5_N:2 * S5_N]
        bu_scr[rows, 0:S5_N] = nr
        bu_scr[rows, S5_N:2 * S5_N] = ni
        return nr, ni

    xr, xi = lax.fori_loop(0, L, s5_body, (s5s_ref[:, 0:S5_N], s5s_ref[:, S5_N:2 * S5_N]), unroll=2)
    s5s_ref[:, 0:S5_N] = xr
    s5s_ref[:, S5_N:2 * S5_N] = xi
    y = (_dot(bu_scr[:, 0:S5_N].astype(BF16), cre_ref[...])
         - _dot(bu_scr[:, S5_N:2 * S5_N].astype(BF16), cim_ref[...]) + s5d_ref[...] * u)
    zz = jax.nn.gelu(y)
    glu = zz * jax.nn.sigmoid(_dot(zz.astype(BF16), gluw_ref[...]) + glub_ref[...])
    mix_scr[:, D_RET + D_LRU:] = _rms(glu, mnw[:, D_RET + D_LRU:]).astype(BF16)

    xo_ref[...] = x_ref[...] + _dot(mix_scr[...], wout_ref[...])


def _const_spec(shape):
    nd = len(shape)
    return pl.BlockSpec(shape, lambda g, c: (0,) * nd)


def _group_spec(shape):
    nd = len(shape)
    return pl.BlockSpec((None,) + shape, lambda g, c: (g,) + (0,) * nd)


def _mixer(xg, cos2, sin2, st_ret, st_lh, st_lc, st_s5, p):
    ng, rows, _ = xg.shape
    nc = rows // R
    x_spec = pl.BlockSpec((None, R, D_MODEL), lambda g, c: (g, c, 0))
    t_spec = pl.BlockSpec((L, LANES), lambda g, c: (c, 0))
    consts = (_DMASK, _CDEC, _KDEC)
    weights = (p["n1w"], p["w_in"], p["mnw"], p["lcw"], p["lcb"], p["wg"], p["gb"], p["lam"],
               p["ab"], p["bmat"], p["cre"], p["cim"], p["s5d"], p["gluw"], p["glub"], p["w_out"])
    state_shapes = ((G, RET_HEADS, RET_DIM, RET_DIM), (G, D_LRU), ((LRU_CONV - 1) * G, D_LRU), (G, 2 * S5_N))
    return pl.pallas_call(
        _mixer_kernel,
        grid=(ng, nc),
        in_specs=[x_spec, t_spec, t_spec] + [_group_spec(s) for s in state_shapes]
                 + [_const_spec(a.shape) for a in consts + weights],
        out_specs=[x_spec] + [_group_spec(s) for s in state_shapes],
        out_shape=[jax.ShapeDtypeStruct(xg.shape, F32)]
                  + [jax.ShapeDtypeStruct((ng,) + s, F32) for s in state_shapes],
        scratch_shapes=[
            pltpu.VMEM((R, D_MODEL), BF16),
            pltpu.VMEM((RET_HEADS, R, RET_DIM), F32),
            pltpu.VMEM((RET_HEADS, R, RET_DIM), F32),
            pltpu.VMEM((RET_HEADS, R, RET_DIM), F32),
            pltpu.VMEM((RET_HEADS, R, RET_DIM), F32),
            pltpu.VMEM((R, D_RET), F32),
            pltpu.VMEM((R + (LRU_CONV - 1) * G, D_LRU), F32),
            pltpu.VMEM((R, D_LRU), F32),
            pltpu.VMEM((R, D_LRU), F32),
            pltpu.VMEM((R, D_LRU), F32),
            pltpu.VMEM((R, D_S5), F32),
            pltpu.VMEM((R, 2 * S5_N), F32),
            pltpu.VMEM((R, D_MODEL), BF16),
        ],
        compiler_params=pltpu.CompilerParams(
            dimension_semantics=("arbitrary", "arbitrary"), vmem_limit_bytes=VMEM_LIMIT),
        name="mixer",
    )(xg, cos2, sin2, st_ret, st_lh, st_lc, st_s5, *consts, *weights)


def _ffn_kernel(x_ref, sfc_ref, n2w_ref, up_ref, cw_ref, cb_ref, down_ref, fnw_ref,
                xo_ref, fc_ref, h_scr, extg_scr, extv_scr, *, final_norm):
    c = pl.program_id(1)
    n_carry = (FFN_CONV - 1) * G

    @pl.when(c == 0)
    def _():
        fc_ref[...] = sfc_ref[...]

    h_scr[...] = _rms(x_ref[...], n2w_ref[...]).astype(BF16)
    hb = h_scr[...]
    acc = x_ref[...]
    for j in range(D_FF // FF_TILE):
        conv = []
        for half, ext in enumerate((extg_scr, extv_scr)):
            cols = slice(half * D_FF + j * FF_TILE, half * D_FF + (j + 1) * FF_TILE)
            ext[0:n_carry, :] = fc_ref[:, cols]
            ext[n_carry:n_carry + R, :] = _dot(hb, up_ref[:, cols])
            fc_ref[:, cols] = ext[R:R + n_carry, :]
            y = cb_ref[:, cols] + cw_ref[0:1, cols] * ext[0:R, :]
            for k in range(1, FFN_CONV):
                y = y + cw_ref[k:k + 1, cols] * ext[k * G:k * G + R, :]
            conv.append(y)
        act = (jax.nn.silu(conv[0]) * conv[1]).astype(BF16)
        acc = acc + _dot(act, down_ref[j * FF_TILE:(j + 1) * FF_TILE, :])
    if final_norm:
        acc = _rms(acc, fnw_ref[...])
    xo_ref[...] = acc


def _ffn(xg, st_fc, p, fnw, final_norm):
    ng, rows, _ = xg.shape
    nc = rows // R
    x_spec = pl.BlockSpec((None, R, D_MODEL), lambda g, c: (g, c, 0))
    fc_shape = ((FFN_CONV - 1) * G, 2 * D_FF)
    weights = (p["n2w"], p["ffn_up"], p["fcw"], p["fcb"], p["ffn_down"], fnw)
    return pl.pallas_call(
        functools.partial(_ffn_kernel, final_norm=final_norm),
        grid=(ng, nc),
        in_specs=[x_spec, _group_spec(fc_shape)] + [_const_spec(a.shape) for a in weights],
        out_specs=[x_spec, _group_spec(fc_shape)],
        out_shape=[jax.ShapeDtypeStruct(xg.shape, F32), jax.ShapeDtypeStruct((ng,) + fc_shape, F32)],
        scratch_shapes=[
            pltpu.VMEM((R, D_MODEL), BF16),
            pltpu.VMEM((R + (FFN_CONV - 1) * G, FF_TILE), F32),
            pltpu.VMEM((R + (FFN_CONV - 1) * G, FF_TILE), F32),
        ],
        compiler_params=pltpu.CompilerParams(
            dimension_semantics=("arbitrary", "arbitrary"), vmem_limit_bytes=VMEM_LIMIT),
        name="convffn",
    )(xg, st_fc, *weights)


def _to_groups(x):
    b, t, c = x.shape
    return x.reshape(b // G, G, t, c).transpose(0, 2, 1, 3).reshape(b // G, t * G, c)


def _from_groups(xg, t):
    ng, _, c = xg.shape
    return xg.reshape(ng, t, G, c).transpose(0, 2, 1, 3).reshape(ng * G, t, c)


def _layer_params(layer, norm1_w, w_in, mix_norm_w, lru_conv_w, lru_conv_b, lru_gate_w, lru_gate_b, lru_lambda,
                  s5_ab, s5_bbre, s5_bbim, s5_C, s5_D, s5_glu_w, s5_glu_b, w_out, norm2_w, ffn_up,
                  ffn_conv_w, ffn_conv_b, ffn_down):
    i = layer
    return {
        "n1w": norm1_w[i][None], "w_in": w_in[i].astype(BF16), "mnw": mix_norm_w[i][None],
        "lcw": lru_conv_w[i], "lcb": lru_conv_b[i][None],
        "wg": _lru_gate_matrix(lru_gate_w[i]).astype(BF16), "gb": lru_gate_b[i].reshape(1, 2 * D_LRU),
        "lam": lru_lambda[i][None], "ab": s5_ab[i],
        "bmat": jnp.concatenate([_s5_in_matrix(s5_bbre[i]), _s5_in_matrix(s5_bbim[i])], axis=1).astype(BF16),
        "cre": _s5_out_matrix(s5_C[i, ..., 0]).astype(BF16), "cim": _s5_out_matrix(s5_C[i, ..., 1]).astype(BF16),
        "s5d": s5_D[i][None], "gluw": s5_glu_w[i].astype(BF16), "glub": s5_glu_b[i][None],
        "w_out": w_out[i].astype(BF16), "n2w": norm2_w[i][None], "ffn_up": ffn_up[i].astype(BF16),
        "fcw": ffn_conv_w[i], "fcb": ffn_conv_b[i][None], "ffn_down": ffn_down[i].astype(BF16),
    }


def _trunk(x, pos0, states, params, fnw):
    b, t, _ = x.shape
    ng = b // G
    xg = _to_groups(x)
    cos2, sin2 = _rope_tables(t, pos0)
    st_ret, st_lh, st_lc, st_s5, st_fc = states
    new = []
    for layer in range(DEPTH):
        p = params[layer]
        xg, ret, lh, lc, s5 = _mixer(xg, cos2, sin2, st_ret[layer], st_lh[layer], st_lc[layer], st_s5[layer], p)
        xg, fc = _ffn(xg, st_fc[layer], p, fnw, final_norm=(layer == DEPTH - 1))
        new.append((ret, lh, lc, s5, fc))
    ret, lh, lc, s5, fc = (jnp.stack([n[i] for n in new]) for i in range(5))
    ret = ret.reshape(DEPTH, b, RET_HEADS, RET_DIM, RET_DIM)
    lh = lh.reshape(DEPTH, b, D_LRU)
    lc = lc.reshape(DEPTH, ng, LRU_CONV - 1, G, D_LRU).transpose(0, 1, 3, 2, 4).reshape(DEPTH, b, LRU_CONV - 1, D_LRU)
    s5 = s5.reshape(DEPTH, b, 2, S5_GROUPS, S5_STATE).transpose(0, 1, 3, 4, 2)
    fc = fc.reshape(DEPTH, ng, FFN_CONV - 1, G, 2 * D_FF).transpose(0, 1, 3, 2, 4).reshape(DEPTH, b, FFN_CONV - 1, 2 * D_FF)
    return _from_groups(xg, t), (ret, lh, lc, s5, fc)


def _states_to_groups(state_ret, state_lru_h, state_lru_conv, state_s5, state_ffn_conv):
    b = state_ret.shape[1]
    ng = b // G
    ret = state_ret.reshape(DEPTH, ng, G, RET_HEADS, RET_DIM, RET_DIM)
    lh = state_lru_h.reshape(DEPTH, ng, G, D_LRU)
    lc = state_lru_conv.reshape(DEPTH, ng, G, LRU_CONV - 1, D_LRU).transpose(0, 1, 3, 2, 4).reshape(
        DEPTH, ng, (LRU_CONV - 1) * G, D_LRU)
    s5 = state_s5.reshape(DEPTH, ng, G, S5_N, 2).transpose(0, 1, 2, 4, 3).reshape(DEPTH, ng, G, 2 * S5_N)
    fc = state_ffn_conv.reshape(DEPTH, ng, G, FFN_CONV - 1, 2 * D_FF).transpose(0, 1, 3, 2, 4).reshape(
        DEPTH, ng, (FFN_CONV - 1) * G, 2 * D_FF)
    return ret, lh, lc, s5, fc


def kernel(x_prompt, x_sample, state_ret, state_lru_h, state_lru_conv, state_s5, state_ffn_conv, norm1_w, w_in, mix_norm_w, lru_conv_w, lru_conv_b, lru_gate_w, lru_gate_b, lru_lambda, s5_lambda_re, s5_lambda_im, s5_log_dt, s5_B, s5_C, s5_D, s5_glu_w, s5_glu_b, w_out, norm2_w, ffn_up, ffn_conv_w, ffn_conv_b, ffn_down, final_norm_w):
    s5_ab, s5_bbre, s5_bbim = _s5_discretize(s5_lambda_re, s5_lambda_im, s5_log_dt, s5_B)
    params = [_layer_params(layer, norm1_w, w_in, mix_norm_w, lru_conv_w, lru_conv_b, lru_gate_w, lru_gate_b,
                            lru_lambda, s5_ab, s5_bbre, s5_bbim, s5_C, s5_D, s5_glu_w, s5_glu_b, w_out, norm2_w,
                            ffn_up, ffn_conv_w, ffn_conv_b, ffn_down) for layer in range(DEPTH)]
    fnw = final_norm_w[None]

    bp = x_prompt.shape[0]
    zero_states = _states_to_groups(
        jnp.zeros((DEPTH, bp, RET_HEADS, RET_DIM, RET_DIM), F32), jnp.zeros((DEPTH, bp, D_LRU), F32),
        jnp.zeros((DEPTH, bp, LRU_CONV - 1, D_LRU), F32), jnp.zeros((DEPTH, bp, S5_GROUPS, S5_STATE, 2), F32),
        jnp.zeros((DEPTH, bp, FFN_CONV - 1, 2 * D_FF), F32))
    y_prompt, new_p = _trunk(x_prompt, 0, zero_states, params, fnw)
    sample_states = _states_to_groups(state_ret, state_lru_h, state_lru_conv, state_s5, state_ffn_conv)
    y_sample, new_s = _trunk(x_sample, PAST_LEN, sample_states, params, fnw)
    return (y_prompt, y_sample) + new_p + new_s
```

```python
import functools
import math

import numpy as np
import jax
import jax.numpy as jnp
from jax import lax
from jax.experimental import pallas as pl
from jax.experimental.pallas import tpu as pltpu

D_MODEL = 1024
DEPTH = 2
PAST_LEN = 2048
RET_HEADS = 4
RET_DIM = 128
D_RET = RET_HEADS * RET_DIM
D_LRU = 256
LRU_BLOCKS = 4
LRU_BLOCK = D_LRU // LRU_BLOCKS
LRU_CONV = 4
LRU_C = 8.0
D_S5 = 256
S5_GROUP = 16
S5_GROUPS = D_S5 // S5_GROUP
S5_STATE = 64
S5_N = S5_GROUPS * S5_STATE
D_FF = 2816
FFN_CONV = 3
ROPE_BASE = 10000.0
EPS = 1e-6
D_IN = 4 * D_RET + 2 * D_LRU + D_S5

G = 8
L = 64
R = L * G
LANES = 128
FF_TILE = D_FF // 2
VMEM_LIMIT = 56 * 1024 * 1024

F32 = jnp.float32
BF16 = jnp.bfloat16

_LOG_G = [math.log1p(-(2.0 ** (-5.0 - h))) for h in range(RET_HEADS)]


def _ret_tables():
    t = (np.arange(R) // G).astype(np.float64)
    b = np.arange(R) % G
    lg = np.asarray(_LOG_G)[:, None]
    qdec = np.exp(lg * (t + 1.0))[:, :, None] * np.ones((1, 1, RET_DIM))
    kdec = np.exp(lg * (L - 1.0 - t))[:, :, None] * np.ones((1, 1, RET_DIM)) * RET_DIM ** -0.5
    causal = (b[:, None] == b[None, :]) & (t[:, None] >= t[None, :])
    return causal.astype(np.float32), qdec.astype(np.float32), kdec.astype(np.float32)


_CAUSAL, _QDEC, _KDEC = _ret_tables()
_SDEC = [math.exp(lg * L) for lg in _LOG_G]
_SINV = [math.exp(-lg * L) for lg in _LOG_G]


def _rms(x, w):
    return x * lax.rsqrt(jnp.mean(x * x, axis=-1, keepdims=True) + EPS) * w


def _dot(a, b):
    return jnp.dot(a, b, preferred_element_type=F32)


def _rope_kernel(cos_ref, sin_ref, *, pos0, rows):
    base = pl.program_id(0) * rows + pos0
    lane = lax.broadcasted_iota(jnp.int32, (rows, LANES), 1)
    row = lax.broadcasted_iota(jnp.int32, (rows, LANES), 0)
    half = RET_DIM // 2
    idx = jnp.where(lane >= half, lane - half, lane).astype(F32)
    freq = jnp.exp(idx * (-math.log(ROPE_BASE) / half))
    ang = (row + base).astype(F32) * freq
    s = jnp.sin(ang)
    cos_ref[...] = jnp.cos(ang)
    sin_ref[...] = jnp.where(lane >= half, s, -s)


def _rope_tables(T, pos0):
    rows = min(T, 256)
    return pl.pallas_call(
        functools.partial(_rope_kernel, pos0=pos0, rows=rows),
        grid=(T // rows,),
        out_specs=[pl.BlockSpec((rows, LANES), lambda i: (i, 0))] * 2,
        out_shape=[jax.ShapeDtypeStruct((T, LANES), F32)] * 2,
        name="rope_tables",
    )()


def _s5_prep_kernel(lre_ref, lim_ref, ldt_ref, bre_ref, bim_ref, ab_ref, bbre_ref, bbim_ref):
    dt = jnp.exp(ldt_ref[...])
    lre = lre_ref[...]
    lim = lim_ref[...]
    mag = jnp.exp(lre * dt)
    ph = lim * dt
    abr = mag * jnp.cos(ph)
    abi = mag * jnp.sin(ph)
    den = lre * lre + lim * lim
    nr = abr - 1.0
    fre = (nr * lre + abi * lim) / den
    fim = (abi * lre - nr * lim) / den
    ab_ref[0:1, :] = abr
    ab_ref[1:2, :] = abi
    bre = bre_ref[...]
    bim = bim_ref[...]
    bbre_ref[...] = fre * bre - fim * bim
    bbim_ref[...] = fre * bim + fim * bre


def _s5_discretize(lam_re, lam_im, log_dt, s5_B):
    lre = lam_re.reshape(DEPTH, 1, S5_N)
    lim = lam_im.reshape(DEPTH, 1, S5_N)
    ldt = jnp.broadcast_to(log_dt[:, :, None], (DEPTH, S5_GROUPS, S5_STATE)).reshape(DEPTH, 1, S5_N)
    bt = s5_B.transpose(0, 3, 1, 2, 4).reshape(DEPTH, S5_GROUP, S5_N, 2)
    row = lambda r: pl.BlockSpec((None, r, S5_N), lambda d: (d, 0, 0))
    return pl.pallas_call(
        _s5_prep_kernel,
        grid=(DEPTH,),
        in_specs=[row(1), row(1), row(1), row(S5_GROUP), row(S5_GROUP)],
        out_specs=[row(2), row(S5_GROUP), row(S5_GROUP)],
        out_shape=[jax.ShapeDtypeStruct((DEPTH, 2, S5_N), F32),
                   jax.ShapeDtypeStruct((DEPTH, S5_GROUP, S5_N), F32),
                   jax.ShapeDtypeStruct((DEPTH, S5_GROUP, S5_N), F32)],
        name="s5_discretize",
    )(lre, lim, ldt, bt[..., 0], bt[..., 1])


def _s5_in_matrix(bb):
    gi = jnp.arange(S5_GROUPS)
    same = gi[:, None, None, None] == gi[None, None, :, None]
    full = jnp.where(same, bb.reshape(1, S5_GROUP, S5_GROUPS, S5_STATE), 0.0)
    return full.reshape(D_S5, S5_N)


def _s5_out_matrix(c):
    gi = jnp.arange(S5_GROUPS)
    same = gi[:, None, None, None] == gi[None, None, :, None]
    full = jnp.where(same, c.transpose(2, 0, 1)[None], 0.0)
    return full.reshape(S5_N, D_S5)


def _lru_gate_matrix(gw):
    bi = jnp.arange(LRU_BLOCKS)
    same = bi[:, None, None, None, None] == bi[None, None, None, :, None]
    wt = gw.transpose(1, 2, 0, 3)[:, :, :, None, :]
    return jnp.where(same, wt, 0.0).reshape(D_LRU, 2 * D_LRU)


def _mixer_kernel(x_ref, cos_ref, sin_ref, sret_ref, slh_ref, slc_ref, ss5_ref,
                  causal_ref, qdec_ref, kdec_ref,
                  n1w_ref, win_ref, mnw_ref, lcw_ref, lcb_ref, wg_ref, gb_ref, lam_ref,
                  ab_ref, bmat_ref, cre_ref, cim_ref, s5d_ref, gluw_ref, glub_ref, wout_ref,
                  xo_ref, ret_ref, lh_ref, lc_ref, s5s_ref,
                  h_scr, q_scr, k_scr, v_scr, scat_scr, g_scr, ext_scr, lgate_scr, a_scr, b_scr,
                  u_scr, bu_scr, mix_scr):
    c = pl.program_id(1)
    n_carry = (LRU_CONV - 1) * G

    @pl.when(c == 0)
    def _():
        for b in range(G):
            for hh in range(RET_HEADS):
                scat_scr[hh, :, b * RET_DIM:(b + 1) * RET_DIM] = sret_ref[b, hh]
        lh_ref[...] = slh_ref[...]
        ext_scr[0:n_carry, :] = slc_ref[...]
        s5s_ref[...] = ss5_ref[...]

    h_scr[...] = _rms(x_ref[...], n1w_ref[...]).astype(BF16)
    hb = h_scr[...]
    for j, dst in enumerate((q_scr, k_scr, v_scr)):
        res = _dot(hb, win_ref[:, j * D_RET:(j + 1) * D_RET])
        for hh in range(RET_HEADS):
            dst[hh] = res[:, hh * RET_DIM:(hh + 1) * RET_DIM]
    g_scr[...] = _dot(hb, win_ref[:, 3 * D_RET:4 * D_RET])
    res = _dot(hb, win_ref[:, 4 * D_RET:4 * D_RET + 2 * D_LRU])
    ext_scr[n_carry:n_carry + R, :] = res[:, :D_LRU]
    lgate_scr[...] = res[:, D_LRU:]
    u_scr[...] = _dot(hb, win_ref[:, 4 * D_RET + 2 * D_LRU:])

    cosv = cos_ref[...]
    sinv = sin_ref[...]
    causal = causal_ref[...]
    mnw = mnw_ref[...]
    stream = lax.broadcasted_iota(jnp.int32, (R, RET_DIM), 0) % G
    for hh in range(RET_HEADS):
        cols = slice(hh * RET_DIM, (hh + 1) * RET_DIM)
        q = q_scr[hh]
        k = k_scr[hh]
        v = v_scr[hh]
        qs = ((q * cosv + pltpu.roll(q, RET_DIM // 2, 1) * sinv) * qdec_ref[hh]).astype(BF16)
        ks = ((k * cosv + pltpu.roll(k, RET_DIM // 2, 1) * sinv) * kdec_ref[hh]).astype(BF16)
        scores = lax.dot_general(qs, ks, (((1,), (1,)), ((), ())), preferred_element_type=F32) * causal
        s_cat = scat_scr[hh]
        cross_all = _dot(qs, s_cat.astype(BF16))
        cross = cross_all[:, 0:RET_DIM]
        for b in range(1, G):
            cross = jnp.where(stream == b, cross_all[:, b * RET_DIM:(b + 1) * RET_DIM], cross)
        o = _dot(scores.astype(BF16), v.astype(BF16)) * _SINV[hh] + cross
        v_by_stream = jnp.concatenate([jnp.where(stream == b, v, 0.0) for b in range(G)], axis=1).astype(BF16)
        scat_scr[hh] = _SDEC[hh] * s_cat + lax.dot_general(
            ks, v_by_stream, (((0,), (0,)), ((), ())), preferred_element_type=F32)
        o = o * lax.rsqrt(jnp.mean(o * o, axis=-1, keepdims=True) + EPS)
        mix_scr[:, cols] = (o * mnw[:, cols] * jax.nn.silu(g_scr[:, cols])).astype(BF16)

    @pl.when(c == pl.num_programs(1) - 1)
    def _():
        for b in range(G):
            for hh in range(RET_HEADS):
                ret_ref[b, hh] = scat_scr[hh, :, b * RET_DIM:(b + 1) * RET_DIM]

    conv = lcb_ref[...] + lcw_ref[0:1, :] * ext_scr[0:R, :]
    for j in range(1, LRU_CONV):
        conv = conv + lcw_ref[j:j + 1, :] * ext_scr[j * G:j * G + R, :]
    carry_rows = ext_scr[R:R + n_carry, :]
    ext_scr[0:n_carry, :] = carry_rows
    lc_ref[...] = carry_rows
    gates = _dot(conv.astype(BF16), wg_ref[...]) + gb_ref[...]
    z = -lam_ref[...]
    softplus = jnp.maximum(z, 0.0) + jnp.log1p(jnp.exp(-jnp.abs(z)))
    log_a = -LRU_C * jax.nn.sigmoid(gates[:, :D_LRU]) * softplus
    a = jnp.exp(log_a)
    a_scr[...] = a
    b_scr[...] = jnp.sqrt(-jnp.tanh(log_a) * (a * a + 1.0)) * (jax.nn.sigmoid(gates[:, D_LRU:]) * conv)

    def lru_body(t, h):
        rows = pl.ds(pl.multiple_of(t * G, G), G)
        h = a_scr[rows, :] * h + b_scr[rows, :]
        b_scr[rows, :] = h
        return h

    lh_ref[...] = lax.fori_loop(0, L, lru_body, lh_ref[...], unroll=8)
    lru_out = _rms(jax.nn.gelu(lgate_scr[...]) * b_scr[...], mnw[:, D_RET:D_RET + D_LRU])
    mix_scr[:, D_RET:D_RET + D_LRU] = lru_out.astype(BF16)

    u = u_scr[...]
    bu_scr[...] = _dot(u.astype(BF16), bmat_ref[...])
    ar = jnp.broadcast_to(ab_ref[0:1, :], (G, S5_N))
    ai = jnp.broadcast_to(ab_ref[1:2, :], (G, S5_N))

    def s5_body(t, carry):
        xr, xi = carry
        rows = pl.ds(pl.multiple_of(t * G, G), G)
        nr = ar * xr - ai * xi + bu_scr[rows, 0:S5_N]
        ni = ar * xi + ai * xr + bu_scr[rows, S5_N:2 * S5_N]
        bu_scr[rows, 0:S5_N] = nr
        bu_scr[rows, S5_N:2 * S5_N] = ni
        return nr, ni

    xr, xi = lax.fori_loop(0, L, s5_body, (s5s_ref[:, 0:S5_N], s5s_ref[:, S5_N:2 * S5_N]), unroll=2)
    s5s_ref[:, 0:S5_N] = xr
    s5s_ref[:, S5_N:2 * S5_N] = xi
    y = (_dot(bu_scr[:, 0:S5_N].astype(BF16), cre_ref[...])
         - _dot(bu_scr[:, S5_N:2 * S5_N].astype(BF16), cim_ref[...]) + s5d_ref[...] * u)
    zz = jax.nn.gelu(y)
    glu = zz * jax.nn.sigmoid(_dot(zz.astype(BF16), gluw_ref[...]) + glub_ref[...])
    mix_scr[:, D_RET + D_LRU:] = _rms(glu, mnw[:, D_RET + D_LRU:]).astype(BF16)

    xo_ref[...] = x_ref[...] + _dot(mix_scr[...], wout_ref[...])


def _const_spec(shape):
    nd = len(shape)
    return pl.BlockSpec(shape, lambda g, c: (0,) * nd)


def _group_spec(shape):
    nd = len(shape)
    return pl.BlockSpec((None,) + shape, lambda g, c: (g,) + (0,) * nd)


def _mixer(xg, cos2, sin2, st_ret, st_lh, st_lc, st_s5, p):
    ng, rows, _ = xg.shape
    nc = rows // R
    x_spec = pl.BlockSpec((None, R, D_MODEL), lambda g, c: (g, c, 0))
    t_spec = pl.BlockSpec((R, LANES), lambda g, c: (c, 0))
    consts = (_CAUSAL, _QDEC, _KDEC)
    weights = (p["n1w"], p["w_in"], p["mnw"], p["lcw"], p["lcb"], p["wg"], p["gb"], p["lam"],
               p["ab"], p["bmat"], p["cre"], p["cim"], p["s5d"], p["gluw"], p["glub"], p["w_out"])
    state_shapes = ((G, RET_HEADS, RET_DIM, RET_DIM), (G, D_LRU), ((LRU_CONV - 1) * G, D_LRU), (G, 2 * S5_N))
    return pl.pallas_call(
        _mixer_kernel,
        grid=(ng, nc),
        in_specs=[x_spec, t_spec, t_spec] + [_group_spec(s) for s in state_shapes]
                 + [_const_spec(a.shape) for a in consts + weights],
        out_specs=[x_spec] + [_group_spec(s) for s in state_shapes],
        out_shape=[jax.ShapeDtypeStruct(xg.shape, F32)]
                  + [jax.ShapeDtypeStruct((ng,) + s, F32) for s in state_shapes],
        scratch_shapes=[
            pltpu.VMEM((R, D_MODEL), BF16),
            pltpu.VMEM((RET_HEADS, R, RET_DIM), F32),
            pltpu.VMEM((RET_HEADS, R, RET_DIM), F32),
            pltpu.VMEM((RET_HEADS, R, RET_DIM), F32),
            pltpu.VMEM((RET_HEADS, RET_DIM, G * RET_DIM), F32),
            pltpu.VMEM((R, D_RET), F32),
            pltpu.VMEM((R + (LRU_CONV - 1) * G, D_LRU), F32),
            pltpu.VMEM((R, D_LRU), F32),
            pltpu.VMEM((R, D_LRU), F32),
            pltpu.VMEM((R, D_LRU), F32),
            pltpu.VMEM((R, D_S5), F32),
            pltpu.VMEM((R, 2 * S5_N), F32),
            pltpu.VMEM((R, D_MODEL), BF16),
        ],
        compiler_params=pltpu.CompilerParams(
            dimension_semantics=("arbitrary", "arbitrary"), vmem_limit_bytes=VMEM_LIMIT),
        name="mixer",
    )(xg, cos2, sin2, st_ret, st_lh, st_lc, st_s5, *consts, *weights)


def _ffn_kernel(x_ref, sfc_ref, n2w_ref, up_ref, cw_ref, cb_ref, down_ref, fnw_ref,
                xo_ref, fc_ref, h_scr, extg_scr, extv_scr, *, final_norm):
    c = pl.program_id(1)
    n_carry = (FFN_CONV - 1) * G

    @pl.when(c == 0)
    def _():
        fc_ref[...] = sfc_ref[...]

    h_scr[...] = _rms(x_ref[...], n2w_ref[...]).astype(BF16)
    hb = h_scr[...]
    acc = x_ref[...]
    for j in range(D_FF // FF_TILE):
        conv = []
        for half, ext in enumerate((extg_scr, extv_scr)):
            cols = slice(half * D_FF + j * FF_TILE, half * D_FF + (j + 1) * FF_TILE)
            ext[0:n_carry, :] = fc_ref[:, cols]
            ext[n_carry:n_carry + R, :] = _dot(hb, up_ref[:, cols])
            fc_ref[:, cols] = ext[R:R + n_carry, :]
            y = cb_ref[:, cols] + cw_ref[0:1, cols] * ext[0:R, :]
            for k in range(1, FFN_CONV):
                y = y + cw_ref[k:k + 1, cols] * ext[k * G:k * G + R, :]
            conv.append(y)
        act = (jax.nn.silu(conv[0]) * conv[1]).astype(BF16)
        acc = acc + _dot(act, down_ref[j * FF_TILE:(j + 1) * FF_TILE, :])
    if final_norm:
        acc = _rms(acc, fnw_ref[...])
    xo_ref[...] = acc


def _ffn(xg, st_fc, p, fnw, final_norm):
    ng, rows, _ = xg.shape
    nc = rows // R
    x_spec = pl.BlockSpec((None, R, D_MODEL), lambda g, c: (g, c, 0))
    fc_shape = ((FFN_CONV - 1) * G, 2 * D_FF)
    weights = (p["n2w"], p["ffn_up"], p["fcw"], p["fcb"], p["ffn_down"], fnw)
    return pl.pallas_call(
        functools.partial(_ffn_kernel, final_norm=final_norm),
        grid=(ng, nc),
        in_specs=[x_spec, _group_spec(fc_shape)] + [_const_spec(a.shape) for a in weights],
        out_specs=[x_spec, _group_spec(fc_shape)],
        out_shape=[jax.ShapeDtypeStruct(xg.shape, F32), jax.ShapeDtypeStruct((ng,) + fc_shape, F32)],
        scratch_shapes=[
            pltpu.VMEM((R, D_MODEL), BF16),
            pltpu.VMEM((R + (FFN_CONV - 1) * G, FF_TILE), F32),
            pltpu.VMEM((R + (FFN_CONV - 1) * G, FF_TILE), F32),
        ],
        compiler_params=pltpu.CompilerParams(
            dimension_semantics=("arbitrary", "arbitrary"), vmem_limit_bytes=VMEM_LIMIT),
        name="convffn",
    )(xg, st_fc, *weights)


def _to_groups(x):
    b, t, c = x.shape
    return x.reshape(b // G, G, t, c).transpose(0, 2, 1, 3).reshape(b // G, t * G, c)


def _from_groups(xg, t):
    ng, _, c = xg.shape
    return xg.reshape(ng, t, G, c).transpose(0, 2, 1, 3).reshape(ng * G, t, c)


def _layer_params(layer, norm1_w, w_in, mix_norm_w, lru_conv_w, lru_conv_b, lru_gate_w, lru_gate_b, lru_lambda,
                  s5_ab, s5_bbre, s5_bbim, s5_C, s5_D, s5_glu_w, s5_glu_b, w_out, norm2_w, ffn_up,
                  ffn_conv_w, ffn_conv_b, ffn_down):
    i = layer
    return {
        "n1w": norm1_w[i][None], "w_in": w_in[i].astype(BF16), "mnw": mix_norm_w[i][None],
        "lcw": lru_conv_w[i], "lcb": lru_conv_b[i][None],
        "wg": _lru_gate_matrix(lru_gate_w[i]).astype(BF16), "gb": lru_gate_b[i].reshape(1, 2 * D_LRU),
        "lam": lru_lambda[i][None], "ab": s5_ab[i],
        "bmat": jnp.concatenate([_s5_in_matrix(s5_bbre[i]), _s5_in_matrix(s5_bbim[i])], axis=1).astype(BF16),
        "cre": _s5_out_matrix(s5_C[i, ..., 0]).astype(BF16), "cim": _s5_out_matrix(s5_C[i, ..., 1]).astype(BF16),
        "s5d": s5_D[i][None], "gluw": s5_glu_w[i].astype(BF16), "glub": s5_glu_b[i][None],
        "w_out": w_out[i].astype(BF16), "n2w": norm2_w[i][None], "ffn_up": ffn_up[i].astype(BF16),
        "fcw": ffn_conv_w[i], "fcb": ffn_conv_b[i][None], "ffn_down": ffn_down[i].astype(BF16),
    }


def _trunk(x, pos0, states, params, fnw):
    b, t, _ = x.shape
    ng = b // G
    xg = _to_groups(x)
    cos2, sin2 = (jnp.repeat(tab, G, axis=0) for tab in _rope_tables(t, pos0))
    st_ret, st_lh, st_lc, st_s5, st_fc = states
    new = []
    for layer in range(DEPTH):
        p = params[layer]
        xg, ret, lh, lc, s5 = _mixer(xg, cos2, sin2, st_ret[layer], st_lh[layer], st_lc[layer], st_s5[layer], p)
        xg, fc = _ffn(xg, st_fc[layer], p, fnw, final_norm=(layer == DEPTH - 1))
        new.append((ret, lh, lc, s5, fc))
    ret, lh, lc, s5, fc = (jnp.stack([n[i] for n in new]) for i in range(5))
    ret = ret.reshape(DEPTH, b, RET_HEADS, RET_DIM, RET_DIM)
    lh = lh.reshape(DEPTH, b, D_LRU)
    lc = lc.reshape(DEPTH, ng, LRU_CONV - 1, G, D_LRU).transpose(0, 1, 3, 2, 4).reshape(DEPTH, b, LRU_CONV - 1, D_LRU)
    s5 = s5.reshape(DEPTH, b, 2, S5_GROUPS, S5_STATE).transpose(0, 1, 3, 4, 2)
    fc = fc.reshape(DEPTH, ng, FFN_CONV - 1, G, 2 * D_FF).transpose(0, 1, 3, 2, 4).reshape(DEPTH, b, FFN_CONV - 1, 2 * D_FF)
    return _from_groups(xg, t), (ret, lh, lc, s5, fc)


def _states_to_groups(state_ret, state_lru_h, state_lru_conv, state_s5, state_ffn_conv):
    b = state_ret.shape[1]
    ng = b // G
    ret = state_ret.reshape(DEPTH, ng, G, RET_HEADS, RET_DIM, RET_DIM)
    lh = state_lru_h.reshape(DEPTH, ng, G, D_LRU)
    lc = state_lru_conv.reshape(DEPTH, ng, G, LRU_CONV - 1, D_LRU).transpose(0, 1, 3, 2, 4).reshape(
        DEPTH, ng, (LRU_CONV - 1) * G, D_LRU)
    s5 = state_s5.reshape(DEPTH, ng, G, S5_N, 2).transpose(0, 1, 2, 4, 3).reshape(DEPTH, ng, G, 2 * S5_N)
    fc = state_ffn_conv.reshape(DEPTH, ng, G, FFN_CONV - 1, 2 * D_FF).transpose(0, 1, 3, 2, 4).reshape(
        DEPTH, ng, (FFN_CONV - 1) * G, 2 * D_FF)
    return ret, lh, lc, s5, fc


def kernel(x_prompt, x_sample, state_ret, state_lru_h, state_lru_conv, state_s5, state_ffn_conv, norm1_w, w_in, mix_norm_w, lru_conv_w, lru_conv_b, lru_gate_w, lru_gate_b, lru_lambda, s5_lambda_re, s5_lambda_im, s5_log_dt, s5_B, s5_C, s5_D, s5_glu_w, s5_glu_b, w_out, norm2_w, ffn_up, ffn_conv_w, ffn_conv_b, ffn_down, final_norm_w):
    s5_ab, s5_bbre, s5_bbim = _s5_discretize(s5_lambda_re, s5_lambda_im, s5_log_dt, s5_B)
    params = [_layer_params(layer, norm1_w, w_in, mix_norm_w, lru_conv_w, lru_conv_b, lru_gate_w, lru_gate_b,
                            lru_lambda, s5_ab, s5_bbre, s5_bbim, s5_C, s5_D, s5_glu_w, s5_glu_b, w_out, norm2_w,
                            ffn_up, ffn_conv_w, ffn_conv_b, ffn_down) for layer in range(DEPTH)]
    fnw = final_norm_w[None]

    bp = x_prompt.shape[0]
    zero_states = _states_to_groups(
        jnp.zeros((DEPTH, bp, RET_HEADS, RET_DIM, RET_DIM), F32), jnp.zeros((DEPTH, bp, D_LRU), F32),
        jnp.zeros((DEPTH, bp, LRU_CONV - 1, D_LRU), F32), jnp.zeros((DEPTH, bp, S5_GROUPS, S5_STATE, 2), F32),
        jnp.zeros((DEPTH, bp, FFN_CONV - 1, 2 * D_FF), F32))
    y_prompt, new_p = _trunk(x_prompt, 0, zero_states, params, fnw)
    sample_states = _states_to_groups(state_ret, state_lru_h, state_lru_conv, state_s5, state_ffn_conv)
    y_sample, new_s = _trunk(x_sample, PAST_LEN, sample_states, params, fnw)
    return (y_prompt, y_sample) + new_p + new_s
```

```python
import functools
import math

import numpy as np
import jax
import jax.numpy as jnp
from jax import lax
from jax.experimental import pallas as pl
from jax.experimental.pallas import tpu as pltpu

D_MODEL = 1024
DEPTH = 2
PAST_LEN = 2048
RET_HEADS = 4
RET_DIM = 128
D_RET = RET_HEADS * RET_DIM
D_LRU = 256
LRU_BLOCKS = 4
LRU_BLOCK = D_LRU // LRU_BLOCKS
LRU_CONV = 4
LRU_C = 8.0
D_S5 = 256
S5_GROUP = 16
S5_GROUPS = D_S5 // S5_GROUP
S5_STATE = 64
S5_N = S5_GROUPS * S5_STATE
D_FF = 2816
FFN_CONV = 3
ROPE_BASE = 10000.0
EPS = 1e-6
D_IN = 4 * D_RET + 2 * D_LRU + D_S5

G = 8
LANES = 128
L = 64
R = L * G
MXU_TILE = 256
FF_TILES = (6 * MXU_TILE, 5 * MXU_TILE)
assert sum(FF_TILES) == D_FF
VMEM_LIMIT = 56 * 1024 * 1024

F32 = jnp.float32
BF16 = jnp.bfloat16

_LOG_G = [math.log1p(-(2.0 ** (-5.0 - h))) for h in range(RET_HEADS)]


def _ret_tables():
    t = (np.arange(R) // G).astype(np.float64)
    b = np.arange(R) % G
    lg = np.asarray(_LOG_G)[:, None]
    qdec = np.exp(lg * (t + 1.0))[:, :, None] * np.ones((1, 1, RET_DIM))
    kdec = np.exp(lg * (L - 1.0 - t))[:, :, None] * np.ones((1, 1, RET_DIM)) * RET_DIM ** -0.5
    causal = (b[:, None] == b[None, :]) & (t[:, None] >= t[None, :])
    return causal.astype(np.float32), qdec.astype(np.float32), kdec.astype(np.float32)


_CAUSAL, _QDEC, _KDEC = _ret_tables()
_SDEC = [math.exp(lg * L) for lg in _LOG_G]
_SINV = [math.exp(-lg * L) for lg in _LOG_G]


def _rms(x, w):
    return x * lax.rsqrt(jnp.mean(x * x, axis=-1, keepdims=True) + EPS) * w


def _dot(a, b):
    return jnp.dot(a, b, preferred_element_type=F32)


def _rope_kernel(cos_ref, sin_ref, *, pos0, rows):
    base = pl.program_id(0) * rows + pos0
    lane = lax.broadcasted_iota(jnp.int32, (rows, LANES), 1)
    row = lax.broadcasted_iota(jnp.int32, (rows, LANES), 0)
    half = RET_DIM // 2
    idx = jnp.where(lane >= half, lane - half, lane).astype(F32)
    freq = jnp.exp(idx * (-math.log(ROPE_BASE) / half))
    ang = (row + base).astype(F32) * freq
    s = jnp.sin(ang)
    cos_ref[...] = jnp.cos(ang)
    sin_ref[...] = jnp.where(lane >= half, s, -s)


def _rope_tables(T, pos0):
    rows = min(T, 256)
    return pl.pallas_call(
        functools.partial(_rope_kernel, pos0=pos0, rows=rows),
        grid=(T // rows,),
        out_specs=[pl.BlockSpec((rows, LANES), lambda i: (i, 0))] * 2,
        out_shape=[jax.ShapeDtypeStruct((T, LANES), F32)] * 2,
        name="rope_tables",
    )()


def _s5_prep_kernel(lre_ref, lim_ref, ldt_ref, bre_ref, bim_ref, ab_ref, bbre_ref, bbim_ref):
    dt = jnp.exp(ldt_ref[...])
    lre = lre_ref[...]
    lim = lim_ref[...]
    mag = jnp.exp(lre * dt)
    ph = lim * dt
    abr = mag * jnp.cos(ph)
    abi = mag * jnp.sin(ph)
    den = lre * lre + lim * lim
    nr = abr - 1.0
    fre = (nr * lre + abi * lim) / den
    fim = (abi * lre - nr * lim) / den
    ab_ref[0:1, :] = abr
    ab_ref[1:2, :] = abi
    bre = bre_ref[...]
    bim = bim_ref[...]
    bbre_ref[...] = fre * bre - fim * bim
    bbim_ref[...] = fre * bim + fim * bre


def _s5_discretize(lam_re, lam_im, log_dt, s5_B):
    lre = lam_re.reshape(DEPTH, 1, S5_N)
    lim = lam_im.reshape(DEPTH, 1, S5_N)
    ldt = jnp.broadcast_to(log_dt[:, :, None], (DEPTH, S5_GROUPS, S5_STATE)).reshape(DEPTH, 1, S5_N)
    bt = s5_B.transpose(0, 3, 1, 2, 4).reshape(DEPTH, S5_GROUP, S5_N, 2)
    row = lambda r: pl.BlockSpec((None, r, S5_N), lambda d: (d, 0, 0))
    return pl.pallas_call(
        _s5_prep_kernel,
        grid=(DEPTH,),
        in_specs=[row(1), row(1), row(1), row(S5_GROUP), row(S5_GROUP)],
        out_specs=[row(2), row(S5_GROUP), row(S5_GROUP)],
        out_shape=[jax.ShapeDtypeStruct((DEPTH, 2, S5_N), F32),
                   jax.ShapeDtypeStruct((DEPTH, S5_GROUP, S5_N), F32),
                   jax.ShapeDtypeStruct((DEPTH, S5_GROUP, S5_N), F32)],
        name="s5_discretize",
    )(lre, lim, ldt, bt[..., 0], bt[..., 1])


def _s5_in_matrix(bb):
    gi = jnp.arange(S5_GROUPS)
    same = gi[:, None, None, None] == gi[None, None, :, None]
    full = jnp.where(same, bb.reshape(DEPTH, 1, S5_GROUP, S5_GROUPS, S5_STATE), 0.0)
    return full.reshape(DEPTH, D_S5, S5_N)


def _s5_out_matrix(c):
    gi = jnp.arange(S5_GROUPS)
    same = gi[:, None, None, None] == gi[None, None, :, None]
    full = jnp.where(same, c.transpose(0, 3, 1, 2)[:, None], 0.0)
    return full.reshape(DEPTH, S5_N, D_S5)


def _lru_gate_matrix(gw):
    bi = jnp.arange(LRU_BLOCKS)
    same = bi[:, None, None, None, None] == bi[None, None, None, :, None]
    wt = gw.transpose(0, 2, 3, 1, 4)[:, :, :, :, None, :]
    return jnp.where(same, wt, 0.0).reshape(DEPTH, D_LRU, 2 * D_LRU)


def _mixer_kernel(x_ref, cos_ref, sin_ref, sret_ref, slh_ref, slc_ref, ss5_ref,
                  causal_ref, qdec_ref, kdec_ref,
                  n1w_ref, win_ref, mnw_ref, lcw_ref, lcb_ref, wg_ref, gb_ref, lam_ref,
                  ab_ref, bmat_ref, cre_ref, cim_ref, s5d_ref, gluw_ref, glub_ref, wout_ref,
                  xo_ref, ret_ref, lh_ref, lc_ref, s5s_ref,
                  h_scr, q_scr, k_scr, v_scr, scat_scr, g_scr, ext_scr, lgate_scr, a_scr, b_scr,
                  u_scr, bu_scr, y_scr, mix_scr):
    c = pl.program_id(1)
    n_carry = (LRU_CONV - 1) * G

    @pl.when(c == 0)
    def _():
        for b in range(G):
            for hh in range(RET_HEADS):
                scat_scr[hh, :, b * RET_DIM:(b + 1) * RET_DIM] = sret_ref[b, hh]
        lh_ref[...] = slh_ref[...]
        ext_scr[0:n_carry, :] = slc_ref[...]
        s5s_ref[...] = ss5_ref[...]

    mnw = mnw_ref[...]
    h_scr[...] = _rms(x_ref[...], n1w_ref[...]).astype(BF16)

    def project(col0, width):
        return _dot(h_scr[...], win_ref[:, col0:col0 + width])

    def project_heads(j, dst):
        res = project(j * D_RET, D_RET)
        for hh in range(RET_HEADS):
            dst[hh] = res[:, hh * RET_DIM:(hh + 1) * RET_DIM]

    res = project(4 * D_RET, 2 * D_LRU)
    ext_scr[n_carry:n_carry + R, :] = res[:, :D_LRU]
    lgate_scr[...] = res[:, D_LRU:]
    u_scr[...] = project(4 * D_RET + 2 * D_LRU, D_S5)
    project_heads(0, q_scr)
    bu_scr[...] = _dot(u_scr[...].astype(BF16), bmat_ref[...])

    conv = lcb_ref[...] + lcw_ref[0:1, :] * ext_scr[0:R, :]
    for j in range(1, LRU_CONV):
        conv = conv + lcw_ref[j:j + 1, :] * ext_scr[j * G:j * G + R, :]
    carry_rows = ext_scr[R:R + n_carry, :]
    ext_scr[0:n_carry, :] = carry_rows
    lc_ref[...] = carry_rows
    gates = _dot(conv.astype(BF16), wg_ref[...]) + gb_ref[...]
    z = -lam_ref[...]
    softplus = jnp.maximum(z, 0.0) + jnp.log1p(jnp.exp(-jnp.abs(z)))
    log_a = -LRU_C * jax.nn.sigmoid(gates[:, :D_LRU]) * softplus
    a = jnp.exp(log_a)
    a_scr[...] = a
    b_scr[...] = jnp.sqrt(-jnp.tanh(log_a) * (a * a + 1.0)) * (jax.nn.sigmoid(gates[:, D_LRU:]) * conv)

    project_heads(1, k_scr)

    h_t = lh_ref[...]
    for t in range(L):
        rows = slice(t * G, (t + 1) * G)
        h_t = a_scr[rows, :] * h_t + b_scr[rows, :]
        b_scr[rows, :] = h_t
    lh_ref[...] = h_t
    lru_out = _rms(jax.nn.gelu(lgate_scr[...]) * b_scr[...], mnw[:, D_RET:D_RET + D_LRU])
    mix_scr[:, D_RET:D_RET + D_LRU] = lru_out.astype(BF16)

    project_heads(2, v_scr)

    ar = jnp.broadcast_to(ab_ref[0:1, :], (G, S5_N))
    ai = jnp.broadcast_to(ab_ref[1:2, :], (G, S5_N))
    xr = s5s_ref[:, 0:S5_N]
    xi = s5s_ref[:, S5_N:2 * S5_N]
    for t in range(L):
        rows = slice(t * G, (t + 1) * G)
        xr, xi = (ar * xr - ai * xi + bu_scr[rows, 0:S5_N],
                  ar * xi + ai * xr + bu_scr[rows, S5_N:2 * S5_N])
        bu_scr[rows, 0:S5_N] = xr
        bu_scr[rows, S5_N:2 * S5_N] = xi
    s5s_ref[:, 0:S5_N] = xr
    s5s_ref[:, S5_N:2 * S5_N] = xi

    g_scr[...] = project(3 * D_RET, D_RET)

    cosv = cos_ref[...]
    sinv = sin_ref[...]
    stream =lax.broadcasted_iota(jnp.int32, (R, RET_DIM), 0) % G

    def retention_head(hh):
        cols = slice(hh * RET_DIM, (hh + 1) * RET_DIM)
        q = q_scr[hh]
        k = k_scr[hh]
        v = v_scr[hh]
        qs = ((q * cosv + pltpu.roll(q, RET_DIM // 2, 1) * sinv) * qdec_ref[hh]).astype(BF16)
        ks = ((k * cosv + pltpu.roll(k, RET_DIM // 2, 1) * sinv) * kdec_ref[hh]).astype(BF16)
        vb = v.astype(BF16)
        inner = []
        for r0, n_keys in ((0, R // 2), (R // 2, R)):
            scores = lax.dot_general(qs[r0:r0 + R // 2], ks[0:n_keys], (((1,), (1,)), ((), ())),
                                     preferred_element_type=F32) * causal_ref[r0:r0 + R // 2, 0:n_keys]
            inner.append(_dot(scores.astype(BF16), vb[0:n_keys]))
        s_cat = scat_scr[hh]
        cross_all = _dot(qs, s_cat.astype(BF16))
        cross = cross_all[:, 0:RET_DIM]
        for b in range(1, G):
            cross = jnp.where(stream == b, cross_all[:, b * RET_DIM:(b + 1) * RET_DIM], cross)
        o = jnp.concatenate(inner, axis=0) * _SINV[hh] + cross
        v_by_stream = jnp.concatenate([jnp.where(stream == b, v, 0.0) for b in range(G)], axis=1).astype(BF16)
        scat_scr[hh] = _SDEC[hh] * s_cat + lax.dot_general(
            ks, v_by_stream, (((0,), (0,)), ((), ())), preferred_element_type=F32)
        o = o * lax.rsqrt(jnp.mean(o * o, axis=-1, keepdims=True) + EPS)
        mix_scr[:, cols] = (o * mnw[:, cols] * jax.nn.silu(g_scr[:, cols])).astype(BF16)

    retention_head(0)
    y_scr[...] = (_dot(bu_scr[:, 0:S5_N].astype(BF16), cre_ref[...])
                  - _dot(bu_scr[:, S5_N:2 * S5_N].astype(BF16), cim_ref[...]) + s5d_ref[...] * u_scr[...])
    retention_head(1)
    zz = jax.nn.gelu(y_scr[...])
    glu = zz * jax.nn.sigmoid(_dot(zz.astype(BF16), gluw_ref[...]) + glub_ref[...])
    y_scr[...] = glu
    mix_scr[:, D_RET + D_LRU:] = _rms(y_scr[...], mnw[:, D_RET + D_LRU:]).astype(BF16)

    def out_project(col0, width):
        return _dot(mix_scr[:, col0:col0 + width], wout_ref[col0:col0 + width, :])

    xo_ref[...] = x_ref[...] + out_project(D_RET, D_LRU + D_S5)
    xo_ref[...] += out_project(0, D_RET // 2)
    retention_head(2)
    retention_head(3)

    @pl.when(c == pl.num_programs(1) - 1)
    def _():
        for b in range(G):
            for hh in range(RET_HEADS):
                ret_ref[b, hh] = scat_scr[hh, :, b * RET_DIM:(b + 1) * RET_DIM]

    xo_ref[...] += out_project(D_RET // 2, D_RET // 2)


def _const_spec(shape):
    nd = len(shape)
    return pl.BlockSpec(shape, lambda g, c: (0,) * nd)


def _layer_spec(shape, layer):
    nd = len(shape) - 1
    return pl.BlockSpec((None,) + tuple(shape[1:]), lambda g, c: (layer,) + (0,) * nd)


def _state_spec(shape, layer=None):
    nd = len(shape)
    if layer is None:
        return pl.BlockSpec((None,) + shape, lambda g, c: (g,) + (0,) * nd)
    return pl.BlockSpec((None, None) + shape, lambda g, c: (layer, g) + (0,) * nd)


_MIXER_WEIGHTS = ("n1w", "w_in", "mnw", "lcw", "lcb", "wg", "gb", "lam", "ab", "bmat", "cre", "cim", "s5d",
                  "gluw", "glub", "w_out")
_FFN_WEIGHTS = ("n2w", "ffn_up", "fcw", "fcb", "ffn_down")


def _mixer(xg, cos2, sin2, st_ret, st_lh, st_lc, st_s5, p, layer):
    ng, rows, _ = xg.shape
    nc = rows // R
    x_spec = pl.BlockSpec((None, R, D_MODEL), lambda g, c: (g, c, 0))
    t_spec = pl.BlockSpec((R, LANES), lambda g, c: (c, 0))
    consts = (_CAUSAL, _QDEC, _KDEC)
    weights = tuple(p[n] for n in _MIXER_WEIGHTS)
    state_shapes = ((G, RET_HEADS, RET_DIM, RET_DIM), (G, D_LRU), ((LRU_CONV - 1) * G, D_LRU), (G, 2 * S5_N))
    return pl.pallas_call(
        _mixer_kernel,
        grid=(ng, nc),
        in_specs=[x_spec, t_spec, t_spec] + [_state_spec(s, layer) for s in state_shapes]
                 + [_const_spec(a.shape) for a in consts] + [_layer_spec(w.shape, layer) for w in weights],
        out_specs=[x_spec] + [_state_spec(s) for s in state_shapes],
        out_shape=[jax.ShapeDtypeStruct(xg.shape, F32)]
                  + [jax.ShapeDtypeStruct((ng,) + s, F32) for s in state_shapes],
        scratch_shapes=[
            pltpu.VMEM((R, D_MODEL), BF16),
            pltpu.VMEM((RET_HEADS, R, RET_DIM), F32),
            pltpu.VMEM((RET_HEADS, R, RET_DIM), F32),
            pltpu.VMEM((RET_HEADS, R, RET_DIM), F32),
            pltpu.VMEM((RET_HEADS, RET_DIM, G * RET_DIM), F32),
            pltpu.VMEM((R, D_RET), F32),
            pltpu.VMEM((R + (LRU_CONV - 1) * G, D_LRU), F32),
            pltpu.VMEM((R, D_LRU), F32),
            pltpu.VMEM((R, D_LRU), F32),
            pltpu.VMEM((R, D_LRU), F32),
            pltpu.VMEM((R, D_S5), F32),
            pltpu.VMEM((R, 2 * S5_N), F32),
            pltpu.VMEM((R, D_S5), F32),
            pltpu.VMEM((R, D_MODEL), BF16),
        ],
        compiler_params=pltpu.CompilerParams(
            dimension_semantics=("arbitrary", "arbitrary"), vmem_limit_bytes=VMEM_LIMIT),
        name="mixer",
    )(xg, cos2, sin2, st_ret, st_lh, st_lc, st_s5, *consts, *weights)


def _ffn_kernel(x_ref, sfc_ref, n2w_ref, up_ref, cw_ref, cb_ref, down_ref, fnw_ref,
                xo_ref, fc_ref, h_scr, extg_scr, extv_scr, *maybe_y_scr, last_layer):
    c = pl.program_id(1)
    n_carry = (FFN_CONV - 1) * G
    rows = x_ref.shape[0]

    @pl.when(c == 0)
    def _():
        fc_ref[...] = sfc_ref[...]

    h_scr[...] = _rms(x_ref[...], n2w_ref[...]).astype(BF16)
    acc = x_ref[...]
    col0 = 0
    for width in FF_TILES:
        conv = []
        for half, ext in enumerate((extg_scr, extv_scr)):
            cols = slice(half * D_FF + col0, half * D_FF + col0 + width)
            ext[0:n_carry, 0:width] = fc_ref[:, cols]
            ext[n_carry:n_carry + rows, 0:width] = _dot(h_scr[...], up_ref[:, cols])
            fc_ref[:, cols] = ext[rows:rows + n_carry, 0:width]
            y = cb_ref[:, cols] + cw_ref[0:1, cols] * ext[0:rows, 0:width]
            for k in range(1, FFN_CONV):
                y = y + cw_ref[k:k + 1, cols] * ext[k * G:k * G + rows, 0:width]
            conv.append(y)
        act = (jax.nn.silu(conv[0]) * conv[1]).astype(BF16)
        acc = acc + _dot(act, down_ref[col0:col0 + width, :])
        col0 += width
    if not last_layer:
        xo_ref[...] = acc
        return
    y_scr, = maybe_y_scr
    acc = _rms(acc, fnw_ref[...])
    for j in range(D_MODEL // LANES):
        y_scr[j] = acc[:, j * LANES:(j + 1) * LANES]
    steps = rows // G
    for b in range(G):
        for j in range(D_MODEL // LANES):
            xo_ref[b, :, j * LANES:(j + 1) * LANES] = y_scr[j, pl.ds(b, steps, stride=G), :]


def _ffn(xg, st_fc, p, fnw, layer, batch):
    ng, total_rows, _ = xg.shape
    steps, rows = L, R
    nc = total_rows // rows
    last_layer = layer == DEPTH - 1
    x_spec = pl.BlockSpec((None, rows, D_MODEL), lambda g, c: (g, c, 0))
    fc_shape = ((FFN_CONV - 1) * G, 2 * D_FF)
    weights = tuple(p[n] for n in _FFN_WEIGHTS)
    scratch = [
        pltpu.VMEM((rows, D_MODEL), BF16),
        pltpu.VMEM((rows + (FFN_CONV - 1) * G, max(FF_TILES)), F32),
        pltpu.VMEM((rows + (FFN_CONV - 1) * G, max(FF_TILES)), F32),
    ]
    if last_layer:
        out_x_spec = pl.BlockSpec((G, steps, D_MODEL), lambda g, c: (g, c, 0))
        out_x_shape = jax.ShapeDtypeStruct((batch, total_rows // G, D_MODEL), F32)
        scratch.append(pltpu.VMEM((D_MODEL // LANES, rows, LANES), F32))
    else:
        out_x_spec, out_x_shape = x_spec, jax.ShapeDtypeStruct(xg.shape, F32)
    return pl.pallas_call(
        functools.partial(_ffn_kernel, last_layer=last_layer),
        grid=(ng, nc),
        in_specs=[x_spec, _state_spec(fc_shape, layer)] + [_layer_spec(w.shape, layer) for w in weights]
                 + [_const_spec(fnw.shape)],
        out_specs=[out_x_spec, _state_spec(fc_shape)],
        out_shape=[out_x_shape, jax.ShapeDtypeStruct((ng,) + fc_shape, F32)],
        scratch_shapes=scratch,
        compiler_params=pltpu.CompilerParams(
            dimension_semantics=("arbitrary", "arbitrary"), vmem_limit_bytes=VMEM_LIMIT),
        name="convffn",
    )(xg, st_fc, *weights, fnw)


def _to_groups(x):
    b, t, c = x.shape
    return x.reshape(b // G, G, t, c).transpose(0, 2, 1, 3).reshape(b // G, t * G, c)


def _prepare_params(norm1_w, w_in, mix_norm_w, lru_conv_w, lru_conv_b, lru_gate_w, lru_gate_b, lru_lambda,
                    s5_lambda_re, s5_lambda_im, s5_log_dt, s5_B, s5_C, s5_D, s5_glu_w, s5_glu_b, w_out, norm2_w,
                    ffn_up, ffn_conv_w, ffn_conv_b, ffn_down):
    s5_ab, s5_bbre, s5_bbim = _s5_discretize(s5_lambda_re, s5_lambda_im, s5_log_dt, s5_B)
    vec = lambda a: a.reshape(DEPTH, 1, -1)
    return {
        "n1w": vec(norm1_w), "w_in": w_in.astype(BF16), "mnw": vec(mix_norm_w),
        "lcw": lru_conv_w, "lcb": vec(lru_conv_b),
        "wg": _lru_gate_matrix(lru_gate_w).astype(BF16), "gb": vec(lru_gate_b),
        "lam": vec(lru_lambda), "ab": s5_ab,
        "bmat": jnp.concatenate([_s5_in_matrix(s5_bbre), _s5_in_matrix(s5_bbim)], axis=2).astype(BF16),
        "cre": _s5_out_matrix(s5_C[..., 0]).astype(BF16), "cim": _s5_out_matrix(s5_C[..., 1]).astype(BF16),
        "s5d": vec(s5_D), "gluw": s5_glu_w.astype(BF16), "glub": vec(s5_glu_b),
        "w_out": w_out.astype(BF16), "n2w": vec(norm2_w), "ffn_up": ffn_up.astype(BF16),
        "fcw": ffn_conv_w, "fcb": vec(ffn_conv_b), "ffn_down": ffn_down.astype(BF16),
    }


def _trunk(x, pos0, states, p, fnw):
    b, t, _ = x.shape
    ng = b // G
    xg = _to_groups(x)
    cos2, sin2 = (jnp.repeat(tab, G, axis=0) for tab in _rope_tables(t, pos0))
    st_ret, st_lh, st_lc, st_s5, st_fc = states
    new = []
    for layer in range(DEPTH):
        xg, ret, lh, lc, s5 = _mixer(xg, cos2, sin2, st_ret, st_lh, st_lc, st_s5, p, layer)
        xg, fc = _ffn(xg, st_fc, p, fnw, layer, b)
        new.append((ret, lh, lc, s5, fc))
    ret, lh, lc, s5, fc = (jnp.stack([n[i] for n in new]) for i in range(5))
    ret = ret.reshape(DEPTH, b, RET_HEADS, RET_DIM, RET_DIM)
    lh = lh.reshape(DEPTH, b, D_LRU)
    lc = lc.reshape(DEPTH, ng, LRU_CONV - 1, G, D_LRU).transpose(0, 1, 3, 2, 4).reshape(DEPTH, b, LRU_CONV - 1, D_LRU)
    s5 = s5.reshape(DEPTH, b, 2, S5_GROUPS, S5_STATE).transpose(0, 1, 3, 4, 2)
    fc = fc.reshape(DEPTH, ng, FFN_CONV - 1, G, 2 * D_FF).transpose(0, 1, 3, 2, 4).reshape(DEPTH, b, FFN_CONV - 1, 2 * D_FF)
    return xg, (ret, lh, lc, s5, fc)


def _states_to_groups(state_ret, state_lru_h, state_lru_conv, state_s5, state_ffn_conv):
    b = state_ret.shape[1]
    ng = b // G
    ret = state_ret.reshape(DEPTH, ng, G, RET_HEADS, RET_DIM, RET_DIM)
    lh = state_lru_h.reshape(DEPTH, ng, G, D_LRU)
    lc = state_lru_conv.reshape(DEPTH, ng, G, LRU_CONV - 1, D_LRU).transpose(0, 1, 3, 2, 4).reshape(
        DEPTH, ng, (LRU_CONV - 1) * G, D_LRU)
    s5 = state_s5.reshape(DEPTH, ng, G, S5_N, 2).transpose(0, 1, 2, 4, 3).reshape(DEPTH, ng, G, 2 * S5_N)
    fc = state_ffn_conv.reshape(DEPTH, ng, G, FFN_CONV - 1, 2 * D_FF).transpose(0, 1, 3, 2, 4).reshape(
        DEPTH, ng, (FFN_CONV - 1) * G, 2 * D_FF)
    return ret, lh, lc, s5, fc


def kernel(x_prompt, x_sample, state_ret, state_lru_h, state_lru_conv, state_s5, state_ffn_conv, norm1_w, w_in, mix_norm_w, lru_conv_w, lru_conv_b, lru_gate_w, lru_gate_b, lru_lambda, s5_lambda_re, s5_lambda_im, s5_log_dt, s5_B, s5_C, s5_D, s5_glu_w, s5_glu_b, w_out, norm2_w, ffn_up, ffn_conv_w, ffn_conv_b, ffn_down, final_norm_w):
    p = _prepare_params(norm1_w, w_in, mix_norm_w, lru_conv_w, lru_conv_b, lru_gate_w, lru_gate_b, lru_lambda,
                        s5_lambda_re, s5_lambda_im, s5_log_dt, s5_B, s5_C, s5_D, s5_glu_w, s5_glu_b, w_out, norm2_w,
                        ffn_up, ffn_conv_w, ffn_conv_b, ffn_down)
    fnw = final_norm_w[None]

    bp = x_prompt.shape[0]
    zero_states = _states_to_groups(
        jnp.zeros((DEPTH, bp, RET_HEADS, RET_DIM, RET_DIM), F32), jnp.zeros((DEPTH, bp, D_LRU), F32),
        jnp.zeros((DEPTH, bp, LRU_CONV - 1, D_LRU), F32), jnp.zeros((DEPTH, bp, S5_GROUPS, S5_STATE, 2), F32),
        jnp.zeros((DEPTH, bp, FFN_CONV - 1, 2 * D_FF), F32))
    y_prompt, new_p = _trunk(x_prompt, 0, zero_states, p, fnw)
    sample_states = _states_to_groups(state_ret, state_lru_h, state_lru_conv, state_s5, state_ffn_conv)
    y_sample, new_s = _trunk(x_sample, PAST_LEN, sample_states, p, fnw)
    return (y_prompt, y_sample) + new_p + new_s
```

```python
import functools
import math

import numpy as np
import jax
import jax.numpy as jnp
from jax import lax
from jax.experimental import pallas as pl
from jax.experimental.pallas import tpu as pltpu

D_MODEL = 1024
DEPTH = 2
PAST_LEN = 2048
RET_HEADS = 4
RET_DIM = 128
D_RET = RET_HEADS * RET_DIM
D_LRU = 256
LRU_BLOCKS = 4
LRU_BLOCK = D_LRU // LRU_BLOCKS
LRU_CONV = 4
LRU_C = 8.0
D_S5 = 256
S5_GROUP = 16
S5_GROUPS = D_S5 // S5_GROUP
S5_STATE = 64
S5_N = S5_GROUPS * S5_STATE
D_FF = 2816
FFN_CONV = 3
ROPE_BASE = 10000.0
EPS = 1e-6
D_IN = 4 * D_RET + 2 * D_LRU + D_S5

G = 8
LANES = 128
L = 64
R = L * G
MXU_TILE = 256
FF_TILES = (6 * MXU_TILE, 5 * MXU_TILE)
assert sum(FF_TILES) == D_FF
VMEM_LIMIT = 56 * 1024 * 1024

F32 = jnp.float32
BF16 = jnp.bfloat16

_LOG_G = [math.log1p(-(2.0 ** (-5.0 - h))) for h in range(RET_HEADS)]


def _ret_tables():
    t = (np.arange(R) // G).astype(np.float64)
    b = np.arange(R) % G
    lg = np.asarray(_LOG_G)[:, None]
    qdec = np.exp(lg * (t + 1.0))[:, :, None] * np.ones((1, 1, RET_DIM))
    kdec = np.exp(lg * (L - 1.0 - t))[:, :, None] * np.ones((1, 1, RET_DIM)) * RET_DIM ** -0.5
    causal = (b[:, None] == b[None, :]) & (t[:, None] >= t[None, :])
    return causal.astype(np.float32), qdec.astype(np.float32), kdec.astype(np.float32)


_CAUSAL, _QDEC, _KDEC = _ret_tables()
_SDEC = [math.exp(lg * L) for lg in _LOG_G]
_SINV = [math.exp(-lg * L) for lg in _LOG_G]


def _rms(x, w):
    return x * lax.rsqrt(jnp.mean(x * x, axis=-1, keepdims=True) + EPS) * w


def _dot(a, b):
    return jnp.dot(a, b, preferred_element_type=F32)


def _rope_kernel(cos_ref, sin_ref, *, pos0, rows):
    base = pl.program_id(0) * rows + pos0
    lane = lax.broadcasted_iota(jnp.int32, (rows, LANES), 1)
    row = lax.broadcasted_iota(jnp.int32, (rows, LANES), 0)
    half = RET_DIM // 2
    idx = jnp.where(lane >= half, lane - half, lane).astype(F32)
    freq = jnp.exp(idx * (-math.log(ROPE_BASE) / half))
    ang = (row + base).astype(F32) * freq
    s = jnp.sin(ang)
    cos_ref[...] = jnp.cos(ang)
    sin_ref[...] = jnp.where(lane >= half, s, -s)


def _rope_tables(T, pos0):
    rows = min(T, 256)
    return pl.pallas_call(
        functools.partial(_rope_kernel, pos0=pos0, rows=rows),
        grid=(T // rows,),
        out_specs=[pl.BlockSpec((rows, LANES), lambda i: (i, 0))] * 2,
        out_shape=[jax.ShapeDtypeStruct((T, LANES), F32)] * 2,
        name="rope_tables",
    )()


def _s5_prep_kernel(lre_ref, lim_ref, ldt_ref, bre_ref, bim_ref, ab_ref, bbre_ref, bbim_ref):
    dt = jnp.exp(ldt_ref[...])
    lre = lre_ref[...]
    lim = lim_ref[...]
    mag = jnp.exp(lre * dt)
    ph = lim * dt
    abr = mag * jnp.cos(ph)
    abi = mag * jnp.sin(ph)
    den = lre * lre + lim * lim
    nr = abr - 1.0
    fre = (nr * lre + abi * lim) / den
    fim = (abi * lre - nr * lim) / den
    ab_ref[0:1, :] = abr
    ab_ref[1:2, :] = abi
    bre = bre_ref[...]
    bim = bim_ref[...]
    bbre_ref[...] = fre * bre - fim * bim
    bbim_ref[...] = fre * bim + fim * bre


def _s5_discretize(lam_re, lam_im, log_dt, s5_B):
    lre = lam_re.reshape(DEPTH, 1, S5_N)
    lim = lam_im.reshape(DEPTH, 1, S5_N)
    ldt = jnp.broadcast_to(log_dt[:, :, None], (DEPTH, S5_GROUPS, S5_STATE)).reshape(DEPTH, 1, S5_N)
    bt = s5_B.transpose(0, 3, 1, 2, 4).reshape(DEPTH, S5_GROUP, S5_N, 2)
    row = lambda r: pl.BlockSpec((None, r, S5_N), lambda d: (d, 0, 0))
    return pl.pallas_call(
        _s5_prep_kernel,
        grid=(DEPTH,),
        in_specs=[row(1), row(1), row(1), row(S5_GROUP), row(S5_GROUP)],
        out_specs=[row(2), row(S5_GROUP), row(S5_GROUP)],
        out_shape=[jax.ShapeDtypeStruct((DEPTH, 2, S5_N), F32),
                   jax.ShapeDtypeStruct((DEPTH, S5_GROUP, S5_N), F32),
                   jax.ShapeDtypeStruct((DEPTH, S5_GROUP, S5_N), F32)],
        name="s5_discretize",
    )(lre, lim, ldt, bt[..., 0], bt[..., 1])


def _s5_in_matrix(bb):
    gi = jnp.arange(S5_GROUPS)
    same = gi[:, None, None, None] == gi[None, None, :, None]
    full = jnp.where(same, bb.reshape(DEPTH, 1, S5_GROUP, S5_GROUPS, S5_STATE), 0.0)
    return full.reshape(DEPTH, D_S5, S5_N)


def _s5_out_matrix(c):
    gi = jnp.arange(S5_GROUPS)
    same = gi[:, None, None, None] == gi[None, None, :, None]
    full = jnp.where(same, c.transpose(0, 3, 1, 2)[:, None], 0.0)
    return full.reshape(DEPTH, S5_N, D_S5)


def _lru_gate_matrix(gw):
    bi = jnp.arange(LRU_BLOCKS)
    same = bi[:, None, None, None, None] == bi[None, None, None, :, None]
    wt = gw.transpose(0, 2, 3, 1, 4)[:, :, :, :, None, :]
    return jnp.where(same, wt, 0.0).reshape(DEPTH, D_LRU, 2 * D_LRU)


def _time_major(x3_ref, slab_scr):
    for b in range(G):
        for j in range(D_MODEL // LANES):
            slab_scr[j, pl.ds(b, L, stride=G), :] = x3_ref[b, :, j * LANES:(j + 1) * LANES]


def _mixer_kernel(x_ref, cos_ref, sin_ref, sret_ref, slh_ref, slc_ref, ss5_ref,
                  causal_ref, qdec_ref, kdec_ref,
                  n1w_ref, win_ref, mnw_ref, lcw_ref, lcb_ref, wg_ref, gb_ref, lam_ref,
                  ab_ref, bmat_ref, cre_ref, cim_ref, s5d_ref, gluw_ref, glub_ref, wout_ref,
                  xo_ref, ret_ref, lh_ref, lc_ref, s5s_ref,
                  h_scr, q_scr, k_scr, v_scr, cross_scr, scat_scr, g_scr, ext_scr, lgate_scr, a_scr, b_scr,
                  u_scr, bu_scr, y_scr, mix_scr, *maybe_xt_scr, first_layer):
    c = pl.program_id(1)
    n_carry = (LRU_CONV - 1) * G

    @pl.when(c == 0)
    def _():
        for b in range(G):
            for hh in range(RET_HEADS):
                scat_scr[hh, :, b * RET_DIM:(b + 1) * RET_DIM] = sret_ref[b, hh]
        lh_ref[...] = slh_ref[...]
        ext_scr[0:n_carry, :] = slc_ref[...]
        s5s_ref[...] = ss5_ref[...]

    if first_layer:
        xt_scr, = maybe_xt_scr
        _time_major(x_ref, xt_scr)

        def x_rows():
            return jnp.concatenate([xt_scr[j] for j in range(D_MODEL // LANES)], axis=1)
    else:
        def x_rows():
            return x_ref[...]

    mnw = mnw_ref[...]
    h_scr[...] = _rms(x_rows(), n1w_ref[...]).astype(BF16)

    def project(col0, width):
        return _dot(h_scr[...], win_ref[:, col0:col0 + width])

    def project_heads(j, dst):
        res = project(j * D_RET, D_RET)
        for hh in range(RET_HEADS):
            dst[hh] = res[:, hh * RET_DIM:(hh + 1) * RET_DIM]

    res = project(4 * D_RET, 2 * D_LRU)
    ext_scr[n_carry:n_carry + R, :] = res[:, :D_LRU]
    lgate_scr[...] = res[:, D_LRU:]
    u_scr[...] = project(4 * D_RET + 2 * D_LRU, D_S5)
    project_heads(0, q_scr)
    bu_scr[...] = _dot(u_scr[...].astype(BF16), bmat_ref[...])

    conv = lcb_ref[...] + lcw_ref[0:1, :] * ext_scr[0:R, :]
    for j in range(1, LRU_CONV):
        conv = conv + lcw_ref[j:j + 1, :] * ext_scr[j * G:j * G + R, :]
    carry_rows = ext_scr[R:R + n_carry, :]
    ext_scr[0:n_carry, :] = carry_rows
    lc_ref[...] = carry_rows
    gates = _dot(conv.astype(BF16), wg_ref[...]) + gb_ref[...]
    z = -lam_ref[...]
    softplus = jnp.maximum(z, 0.0) + jnp.log1p(jnp.exp(-jnp.abs(z)))
    log_a = -LRU_C * jax.nn.sigmoid(gates[:, :D_LRU]) * softplus
    a = jnp.exp(log_a)
    a_scr[...] = a
    b_scr[...] = jnp.sqrt(-jnp.tanh(log_a) * (a * a + 1.0)) * (jax.nn.sigmoid(gates[:, D_LRU:]) * conv)

    project_heads(1, k_scr)

    h_t = lh_ref[...]
    for t in range(L):
        rows = slice(t * G, (t + 1) * G)
        h_t = a_scr[rows, :] * h_t + b_scr[rows, :]
        b_scr[rows, :] = h_t
    lh_ref[...] = h_t
    lru_out = _rms(jax.nn.gelu(lgate_scr[...]) * b_scr[...], mnw[:, D_RET:D_RET + D_LRU])
    mix_scr[:, D_RET:D_RET + D_LRU] = lru_out.astype(BF16)

    project_heads(2, v_scr)

    ar = jnp.broadcast_to(ab_ref[0:1, :], (G, S5_N))
    ai = jnp.broadcast_to(ab_ref[1:2, :], (G, S5_N))
    xr = s5s_ref[:, 0:S5_N]
    xi = s5s_ref[:, S5_N:2 * S5_N]
    for t in range(L):
        rows = slice(t * G, (t + 1) * G)
        xr, xi = (ar * xr - ai * xi + bu_scr[rows, 0:S5_N],
                  ar * xi + ai * xr + bu_scr[rows, S5_N:2 * S5_N])
        bu_scr[rows, 0:S5_N] = xr
        bu_scr[rows, S5_N:2 * S5_N] = xi
    s5s_ref[:, 0:S5_N] = xr
    s5s_ref[:, S5_N:2 * S5_N] = xi

    g_scr[...] = project(3 * D_RET, D_RET)

    cosv = cos_ref[...]
    sinv = sin_ref[...]

    def retention_head(hh):
        cols = slice(hh * RET_DIM, (hh + 1) * RET_DIM)
        q = q_scr[hh]
        k = k_scr[hh]
        v = v_scr[hh]
        qs_f32 = (q * cosv + pltpu.roll(q, RET_DIM // 2, 1) * sinv) * qdec_ref[hh]
        qs = qs_f32.astype(BF16)
        ks_f32 = (k * cosv + pltpu.roll(k, RET_DIM // 2, 1) * sinv) * kdec_ref[hh]
        ks = ks_f32.astype(BF16)
        vb = v.astype(BF16)
        inner = []
        for r0, n_keys in ((0, R // 2), (R // 2, R)):
            scores = lax.dot_general(qs[r0:r0 + R // 2], ks[0:n_keys], (((1,), (1,)), ((), ())),
                                     preferred_element_type=F32) * causal_ref[r0:r0 + R // 2, 0:n_keys]
            inner.append(_dot(scores.astype(BF16), vb[0:n_keys]))
        q_scr[hh] = qs_f32
        k_scr[hh] = ks_f32
        for b in range(G):
            rows = pl.ds(b, L, stride=G)
            lanes = slice(b * RET_DIM, (b + 1) * RET_DIM)
            s_old = scat_scr[hh, :, lanes]
            cross_scr[hh, rows, :] = _dot(q_scr[hh, rows, :].astype(BF16), s_old.astype(BF16))
            scat_scr[hh, :, lanes] = _SDEC[hh] * s_old + lax.dot_general(
                k_scr[hh, rows, :].astype(BF16), v_scr[hh, rows, :].astype(BF16),
                (((0,), (0,)), ((), ())), preferred_element_type=F32)
        o = jnp.concatenate(inner, axis=0) * _SINV[hh] + cross_scr[hh]
        o = o * lax.rsqrt(jnp.mean(o * o, axis=-1, keepdims=True) + EPS)
        mix_scr[:, cols] = (o * mnw[:, cols] * jax.nn.silu(g_scr[:, cols])).astype(BF16)

    retention_head(0)
    y_scr[...] = (_dot(bu_scr[:, 0:S5_N].astype(BF16), cre_ref[...])
                  - _dot(bu_scr[:, S5_N:2 * S5_N].astype(BF16), cim_ref[...]) + s5d_ref[...] * u_scr[...])
    retention_head(1)
    zz = jax.nn.gelu(y_scr[...])
    glu = zz * jax.nn.sigmoid(_dot(zz.astype(BF16), gluw_ref[...]) + glub_ref[...])
    y_scr[...] = glu
    mix_scr[:, D_RET + D_LRU:] = _rms(y_scr[...], mnw[:, D_RET + D_LRU:]).astype(BF16)

    def out_project(col0, width):
        return _dot(mix_scr[:, col0:col0 + width], wout_ref[col0:col0 + width, :])

    xo_ref[...] = x_rows() + out_project(D_RET, D_LRU + D_S5)
    xo_ref[...] += out_project(0, D_RET // 2)
    retention_head(2)
    retention_head(3)

    @pl.when(c == pl.num_programs(1) - 1)
    def _():
        for b in range(G):
            for hh in range(RET_HEADS):
                ret_ref[b, hh] = scat_scr[hh, :, b * RET_DIM:(b + 1) * RET_DIM]

    xo_ref[...] += out_project(D_RET // 2, D_RET // 2)


def _const_spec(shape):
    nd = len(shape)
    return pl.BlockSpec(shape, lambda g, c: (0,) * nd)


def _layer_spec(shape, layer):
    nd = len(shape) - 1
    return pl.BlockSpec((None,) + tuple(shape[1:]), lambda g, c: (layer,) + (0,) * nd)


def _state_spec(shape, layer=None):
    nd = len(shape)
    if layer is None:
        return pl.BlockSpec((None,) + shape, lambda g, c: (g,) + (0,) * nd)
    return pl.BlockSpec((None, None) + shape, lambda g, c: (layer, g) + (0,) * nd)


_MIXER_WEIGHTS = ("n1w", "w_in", "mnw", "lcw", "lcb", "wg", "gb", "lam", "ab", "bmat", "cre", "cim", "s5d",
                  "gluw", "glub", "w_out")
_FFN_WEIGHTS = ("n2w", "ffn_up", "fcw", "fcb", "ffn_down")


def _mixer(x, cos2, sin2, st_ret, st_lh, st_lc, st_s5, p, layer):
    first_layer = layer == 0
    if first_layer:
        ng, nc = x.shape[0] // G, x.shape[1] // L
        in_block = (G, L, D_MODEL)
    else:
        ng, nc = x.shape[0], x.shape[1] // R
        in_block = (None, R, D_MODEL)

    x_spec = pl.BlockSpec((None, R, D_MODEL), lambda g, c: (g, c, 0))
    t_spec = pl.BlockSpec((R, LANES), lambda g, c: (c, 0))
    consts = (_CAUSAL, _QDEC, _KDEC)
    weights = tuple(p[n] for n in _MIXER_WEIGHTS)
    state_shapes = ((G, RET_HEADS, RET_DIM, RET_DIM), (G, D_LRU), ((LRU_CONV - 1) * G, D_LRU), (G, 2 * S5_N))
    return pl.pallas_call(
        functools.partial(_mixer_kernel, first_layer=first_layer),
        grid=(ng, nc),
        in_specs=[pl.BlockSpec(in_block, lambda g, c: (g, c, 0)), t_spec, t_spec]
                 + [_state_spec(s, layer) for s in state_shapes]
                 + [_const_spec(a.shape) for a in consts] + [_layer_spec(w.shape, layer) for w in weights],
        out_specs=[x_spec] + [_state_spec(s) for s in state_shapes],
        out_shape=[jax.ShapeDtypeStruct((ng, nc * R, D_MODEL), F32)]
                  + [jax.ShapeDtypeStruct((ng,) + s, F32) for s in state_shapes],
        scratch_shapes=[
            pltpu.VMEM((R, D_MODEL), BF16),
            pltpu.VMEM((RET_HEADS, R, RET_DIM), F32),
            pltpu.VMEM((RET_HEADS, R, RET_DIM), F32),
            pltpu.VMEM((RET_HEADS, R, RET_DIM), F32),
            pltpu.VMEM((RET_HEADS, R, RET_DIM), F32),
            pltpu.VMEM((RET_HEADS, RET_DIM, G * RET_DIM), F32),
            pltpu.VMEM((R, D_RET), F32),
            pltpu.VMEM((R + (LRU_CONV - 1) * G, D_LRU), F32),
            pltpu.VMEM((R, D_LRU), F32),
            pltpu.VMEM((R, D_LRU), F32),
            pltpu.VMEM((R, D_LRU), F32),
            pltpu.VMEM((R, D_S5), F32),
            pltpu.VMEM((R, 2 * S5_N), F32),
            pltpu.VMEM((R, D_S5), F32),
            pltpu.VMEM((R, D_MODEL), BF16),
        ] + ([pltpu.VMEM((D_MODEL // LANES, R, LANES), F32)] if first_layer else []),
        compiler_params=pltpu.CompilerParams(
            dimension_semantics=("arbitrary", "arbitrary"), vmem_limit_bytes=VMEM_LIMIT),
        name="mixer",
    )(x, cos2, sin2, st_ret, st_lh, st_lc, st_s5, *consts, *weights)


def _ffn_kernel(x_ref, sfc_ref, n2w_ref, up_ref, cw_ref, cb_ref, down_ref, fnw_ref,
                xo_ref, fc_ref, h_scr, extg_scr, extv_scr, *maybe_y_scr, last_layer):
    c = pl.program_id(1)
    n_carry = (FFN_CONV - 1) * G
    rows = x_ref.shape[0]

    @pl.when(c == 0)
    def _():
        fc_ref[...] = sfc_ref[...]

    h_scr[...] = _rms(x_ref[...], n2w_ref[...]).astype(BF16)
    acc = x_ref[...]
    col0 = 0
    for width in FF_TILES:
        halves = [(ext, slice(half * D_FF + col0, half * D_FF + col0 + width))
                  for half, ext in enumerate((extg_scr, extv_scr))]
        for ext, cols in halves:
            ext[0:n_carry, 0:width] = fc_ref[:, cols]
            ext[n_carry:n_carry + rows, 0:width] = _dot(h_scr[...], up_ref[:, cols])
            fc_ref[:, cols] = ext[rows:rows + n_carry, 0:width]
        conv = []
        for ext, cols in halves:
            y = cb_ref[:, cols] + cw_ref[0:1, cols] * ext[0:rows, 0:width]
            for k in range(1, FFN_CONV):
                y = y + cw_ref[k:k + 1, cols] * ext[k * G:k * G + rows, 0:width]
            conv.append(y)
        act = (jax.nn.silu(conv[0]) * conv[1]).astype(BF16)
        acc = acc + _dot(act, down_ref[col0:col0 + width, :])
        col0 += width
    if not last_layer:
        xo_ref[...] = acc
        return
    y_scr, = maybe_y_scr
    acc = _rms(acc, fnw_ref[...])
    for j in range(D_MODEL // LANES):
        y_scr[j] = acc[:, j * LANES:(j + 1) * LANES]
    steps = rows // G
    for b in range(G):
        for j in range(D_MODEL // LANES):
            xo_ref[b, :, j * LANES:(j + 1) * LANES] = y_scr[j, pl.ds(b, steps, stride=G), :]


def _ffn(xg, st_fc, p, fnw, layer, batch):
    ng, total_rows, _ = xg.shape
    steps, rows = L, R
    nc = total_rows // rows
    last_layer = layer == DEPTH - 1
    x_spec = pl.BlockSpec((None, rows, D_MODEL), lambda g, c: (g, c, 0))
    fc_shape = ((FFN_CONV - 1) * G, 2 * D_FF)
    weights = tuple(p[n] for n in _FFN_WEIGHTS)
    scratch = [
        pltpu.VMEM((rows, D_MODEL), BF16),
        pltpu.VMEM((rows + (FFN_CONV - 1) * G, max(FF_TILES)), F32),
        pltpu.VMEM((rows + (FFN_CONV - 1) * G, max(FF_TILES)), F32),
    ]
    if last_layer:
        out_x_spec = pl.BlockSpec((G, steps, D_MODEL), lambda g, c: (g, c, 0))
        out_x_shape = jax.ShapeDtypeStruct((batch, total_rows // G, D_MODEL), F32)
        scratch.append(pltpu.VMEM((D_MODEL // LANES, rows, LANES), F32))
    else:
        out_x_spec, out_x_shape = x_spec, jax.ShapeDtypeStruct(xg.shape, F32)
    return pl.pallas_call(
        functools.partial(_ffn_kernel, last_layer=last_layer),
        grid=(ng, nc),
        in_specs=[x_spec, _state_spec(fc_shape, layer)]
                 + [_layer_spec(w.shape, layer) for w in weights] + [_const_spec(fnw.shape)],
        out_specs=[out_x_spec, _state_spec(fc_shape)],
        out_shape=[out_x_shape, jax.ShapeDtypeStruct((ng,) + fc_shape, F32)],
        scratch_shapes=scratch,
        compiler_params=pltpu.CompilerParams(
            dimension_semantics=("arbitrary", "arbitrary"), vmem_limit_bytes=VMEM_LIMIT),
        name="convffn",
    )(xg, st_fc, *weights, fnw)


def _prepare_params(norm1_w, w_in, mix_norm_w, lru_conv_w, lru_conv_b, lru_gate_w, lru_gate_b, lru_lambda,
                    s5_lambda_re, s5_lambda_im, s5_log_dt, s5_B, s5_C, s5_D, s5_glu_w, s5_glu_b, w_out, norm2_w,
                    ffn_up, ffn_conv_w, ffn_conv_b, ffn_down):
    s5_ab, s5_bbre, s5_bbim = _s5_discretize(s5_lambda_re, s5_lambda_im, s5_log_dt, s5_B)
    vec = lambda a: a.reshape(DEPTH, 1, -1)
    return {
        "n1w": vec(norm1_w), "w_in": w_in.astype(BF16), "mnw": vec(mix_norm_w),
        "lcw": lru_conv_w, "lcb": vec(lru_conv_b),
        "wg": _lru_gate_matrix(lru_gate_w).astype(BF16), "gb": vec(lru_gate_b),
        "lam": vec(lru_lambda), "ab": s5_ab,
        "bmat": jnp.concatenate([_s5_in_matrix(s5_bbre), _s5_in_matrix(s5_bbim)], axis=2).astype(BF16),
        "cre": _s5_out_matrix(s5_C[..., 0]).astype(BF16), "cim": _s5_out_matrix(s5_C[..., 1]).astype(BF16),
        "s5d": vec(s5_D), "gluw": s5_glu_w.astype(BF16), "glub": vec(s5_glu_b),
        "w_out": w_out.astype(BF16), "n2w": vec(norm2_w), "ffn_up": ffn_up.astype(BF16),
        "fcw": ffn_conv_w, "fcb": vec(ffn_conv_b), "ffn_down": ffn_down.astype(BF16),
    }


def _trunk(x, pos0, states, p, fnw):
    b, t, _ = x.shape
    ng = b // G
    xg = x
    cos2, sin2 =(jnp.repeat(tab, G, axis=0) for tab in _rope_tables(t, pos0))
    st_ret, st_lh, st_lc, st_s5, st_fc = states
    new = []
    for layer in range(DEPTH):
        xg, ret, lh, lc, s5 = _mixer(xg, cos2, sin2, st_ret, st_lh, st_lc, st_s5, p, layer)
        xg, fc = _ffn(xg, st_fc, p, fnw, layer, b)
        new.append((ret, lh, lc, s5, fc))
    ret, lh, lc, s5, fc = (jnp.stack([n[i] for n in new]) for i in range(5))
    ret = ret.reshape(DEPTH, b, RET_HEADS, RET_DIM, RET_DIM)
    lh = lh.reshape(DEPTH, b, D_LRU)
    lc = lc.reshape(DEPTH, ng, LRU_CONV - 1, G, D_LRU).transpose(0, 1, 3, 2, 4).reshape(DEPTH, b, LRU_CONV - 1, D_LRU)
    s5 = s5.reshape(DEPTH, b, 2, S5_GROUPS, S5_STATE).transpose(0, 1, 3, 4, 2)
    fc = fc.reshape(DEPTH, ng, FFN_CONV - 1, G, 2 * D_FF).transpose(0, 1, 3, 2, 4).reshape(DEPTH, b, FFN_CONV - 1, 2 * D_FF)
    return xg, (ret, lh, lc, s5, fc)


def _states_to_groups(state_ret, state_lru_h, state_lru_conv, state_s5, state_ffn_conv):
    b = state_ret.shape[1]
    ng = b // G
    ret = state_ret.reshape(DEPTH, ng, G, RET_HEADS, RET_DIM, RET_DIM)
    lh = state_lru_h.reshape(DEPTH, ng, G, D_LRU)
    lc = state_lru_conv.reshape(DEPTH, ng, G, LRU_CONV - 1, D_LRU).transpose(0, 1, 3, 2, 4).reshape(
        DEPTH, ng, (LRU_CONV - 1) * G, D_LRU)
    s5 = state_s5.reshape(DEPTH, ng, G, S5_N, 2).transpose(0, 1, 2, 4, 3).reshape(DEPTH, ng, G, 2 * S5_N)
    fc = state_ffn_conv.reshape(DEPTH, ng, G, FFN_CONV - 1, 2 * D_FF).transpose(0, 1, 3, 2, 4).reshape(
        DEPTH, ng, (FFN_CONV - 1) * G, 2 * D_FF)
    return ret, lh, lc, s5, fc


def kernel(x_prompt, x_sample, state_ret, state_lru_h, state_lru_conv, state_s5, state_ffn_conv, norm1_w, w_in, mix_norm_w, lru_conv_w, lru_conv_b, lru_gate_w, lru_gate_b, lru_lambda, s5_lambda_re, s5_lambda_im, s5_log_dt, s5_B, s5_C, s5_D, s5_glu_w, s5_glu_b, w_out, norm2_w, ffn_up, ffn_conv_w, ffn_conv_b, ffn_down, final_norm_w):
    p = _prepare_params(norm1_w, w_in, mix_norm_w, lru_conv_w, lru_conv_b, lru_gate_w, lru_gate_b, lru_lambda,
                        s5_lambda_re, s5_lambda_im, s5_log_dt, s5_B, s5_C, s5_D, s5_glu_w, s5_glu_b, w_out, norm2_w,
                        ffn_up, ffn_conv_w, ffn_conv_b, ffn_down)
    fnw = final_norm_w[None]

    bp = x_prompt.shape[0]
    zero_states = _states_to_groups(
        jnp.zeros((DEPTH, bp, RET_HEADS, RET_DIM, RET_DIM), F32), jnp.zeros((DEPTH, bp, D_LRU), F32),
        jnp.zeros((DEPTH, bp, LRU_CONV - 1, D_LRU), F32), jnp.zeros((DEPTH, bp, S5_GROUPS, S5_STATE, 2), F32),
        jnp.zeros((DEPTH, bp, FFN_CONV - 1, 2 * D_FF), F32))
    y_prompt, new_p = _trunk(x_prompt, 0, zero_states, p, fnw)
    sample_states = _states_to_groups(state_ret, state_lru_h, state_lru_conv, state_s5, state_ffn_conv)
    y_sample, new_s = _trunk(x_sample, PAST_LEN, sample_states, p, fnw)
    return (y_prompt, y_sample) + new_p + new_s
```

```python
import functools
import math

import numpy as np
import jax
import jax.numpy as jnp
from jax import lax
from jax.experimental import pallas as pl
from jax.experimental.pallas import tpu as pltpu

D_MODEL = 1024
DEPTH = 2
PAST_LEN = 2048
RET_HEADS = 4
RET_DIM = 128
D_RET = RET_HEADS * RET_DIM
D_LRU = 256
LRU_BLOCKS = 4
LRU_BLOCK = D_LRU // LRU_BLOCKS
LRU_CONV = 4
LRU_C = 8.0
D_S5 = 256
S5_GROUP = 16
S5_GROUPS = D_S5 // S5_GROUP
S5_STATE = 64
S5_N = S5_GROUPS * S5_STATE
D_FF = 2816
FFN_CONV = 3
ROPE_BASE = 10000.0
EPS = 1e-6
D_IN = 4 * D_RET + 2 * D_LRU + D_S5

G = 8
LANES = 128
L = 64
R = L * G
MXU_TILE = 256
L_FFN = 128
FF_TILES = (4 * MXU_TILE, 4 * MXU_TILE, 3 * MXU_TILE)
assert sum(FF_TILES) == D_FF
VMEM_LIMIT = 56 * 1024 * 1024

F32 = jnp.float32
BF16 = jnp.bfloat16

_LOG_G = [math.log1p(-(2.0 ** (-5.0 - h))) for h in range(RET_HEADS)]


def _ret_tables():
    t = (np.arange(R) // G).astype(np.float64)
    b = np.arange(R) % G
    lg = np.asarray(_LOG_G)[:, None]
    qdec = np.exp(lg * (t + 1.0))[:, :, None] * np.ones((1, 1, RET_DIM))
    kdec = np.exp(lg * (L - 1.0 - t))[:, :, None] * np.ones((1, 1, RET_DIM)) * RET_DIM ** -0.5
    causal = (b[:, None] == b[None, :]) & (t[:, None] >= t[None, :])
    return causal.astype(np.float32), qdec.astype(np.float32), kdec.astype(np.float32)


_CAUSAL, _QDEC, _KDEC = _ret_tables()
_SDEC = [math.exp(lg * L) for lg in _LOG_G]
_SINV = [math.exp(-lg * L) for lg in _LOG_G]


def _rms(x, w):
    return x * lax.rsqrt(jnp.mean(x * x, axis=-1, keepdims=True) + EPS) * w


def _dot(a, b):
    return jnp.dot(a, b, preferred_element_type=F32)


def _rope_kernel(cos_ref, sin_ref, *, pos0, rows):
    base = pl.program_id(0) * rows + pos0
    lane = lax.broadcasted_iota(jnp.int32, (rows, LANES), 1)
    row = lax.broadcasted_iota(jnp.int32, (rows, LANES), 0)
    half = RET_DIM // 2
    idx = jnp.where(lane >= half, lane - half, lane).astype(F32)
    freq = jnp.exp(idx * (-math.log(ROPE_BASE) / half))
    ang = (row + base).astype(F32) * freq
    s = jnp.sin(ang)
    cos_ref[...] = jnp.cos(ang)
    sin_ref[...] = jnp.where(lane >= half, s, -s)


def _rope_tables(T, pos0):
    rows = min(T, 256)
    return pl.pallas_call(
        functools.partial(_rope_kernel, pos0=pos0, rows=rows),
        grid=(T // rows,),
        out_specs=[pl.BlockSpec((rows, LANES), lambda i: (i, 0))] * 2,
        out_shape=[jax.ShapeDtypeStruct((T, LANES), F32)] * 2,
        name="rope_tables",
    )()


def _s5_prep_kernel(lre_ref, lim_ref, ldt_ref, bre_ref, bim_ref, ab_ref, bbre_ref, bbim_ref):
    dt = jnp.exp(ldt_ref[...])
    lre = lre_ref[...]
    lim = lim_ref[...]
    mag = jnp.exp(lre * dt)
    ph = lim * dt
    abr = mag * jnp.cos(ph)
    abi = mag * jnp.sin(ph)
    den = lre * lre + lim * lim
    nr = abr - 1.0
    fre = (nr * lre + abi * lim) / den
    fim = (abi * lre - nr * lim) / den
    ab_ref[0:1, :] = abr
    ab_ref[1:2, :] = abi
    bre = bre_ref[...]
    bim = bim_ref[...]
    bbre_ref[...] = fre * bre - fim * bim
    bbim_ref[...] = fre * bim + fim * bre


def _s5_discretize(lam_re, lam_im, log_dt, s5_B):
    lre = lam_re.reshape(DEPTH, 1, S5_N)
    lim = lam_im.reshape(DEPTH, 1, S5_N)
    ldt = jnp.broadcast_to(log_dt[:, :, None], (DEPTH, S5_GROUPS, S5_STATE)).reshape(DEPTH, 1, S5_N)
    bt = s5_B.transpose(0, 3, 1, 2, 4).reshape(DEPTH, S5_GROUP, S5_N, 2)
    row = lambda r: pl.BlockSpec((None, r, S5_N), lambda d: (d, 0, 0))
    return pl.pallas_call(
        _s5_prep_kernel,
        grid=(DEPTH,),
        in_specs=[row(1), row(1), row(1), row(S5_GROUP), row(S5_GROUP)],
        out_specs=[row(2), row(S5_GROUP), row(S5_GROUP)],
        out_shape=[jax.ShapeDtypeStruct((DEPTH, 2, S5_N), F32),
                   jax.ShapeDtypeStruct((DEPTH, S5_GROUP, S5_N), F32),
                   jax.ShapeDtypeStruct((DEPTH, S5_GROUP, S5_N), F32)],
        name="s5_discretize",
    )(lre, lim, ldt, bt[..., 0], bt[..., 1])


def _s5_in_matrix(bb):
    gi = jnp.arange(S5_GROUPS)
    same = gi[:, None, None, None] == gi[None, None, :, None]
    full = jnp.where(same, bb.reshape(DEPTH, 1, S5_GROUP, S5_GROUPS, S5_STATE), 0.0)
    return full.reshape(DEPTH, D_S5, S5_N)


def _s5_out_matrix(c):
    gi = jnp.arange(S5_GROUPS)
    same = gi[:, None, None, None] == gi[None, None, :, None]
    full = jnp.where(same, c.transpose(0, 3, 1, 2)[:, None], 0.0)
    return full.reshape(DEPTH, S5_N, D_S5)


def _lru_gate_matrix(gw):
    bi = jnp.arange(LRU_BLOCKS)
    same = bi[:, None, None, None, None] == bi[None, None, None, :, None]
    wt = gw.transpose(0, 2, 3, 1, 4)[:, :, :, :, None, :]
    return jnp.where(same, wt, 0.0).reshape(DEPTH, D_LRU, 2 * D_LRU)


def _time_major(x3_ref, slab_scr):
    for b in range(G):
        for j in range(D_MODEL // LANES):
            slab_scr[j, pl.ds(b, L, stride=G), :] = x3_ref[b, :, j * LANES:(j + 1) * LANES]


def _mixer_kernel(x_ref, cos_ref, sin_ref, sret_ref, slh_ref, slc_ref, ss5_ref,
                  causal_ref, qdec_ref, kdec_ref,
                  n1w_ref, win_ref, mnw_ref, lcw_ref, lcb_ref, wg_ref, gb_ref, lam_ref,
                  ab_ref, bmat_ref, cre_ref, cim_ref, s5d_ref, gluw_ref, glub_ref, wout_ref,
                  xo_ref, ret_ref, lh_ref, lc_ref, s5s_ref,
                  h_scr, q_scr, k_scr, v_scr, cross_scr, scat_scr, g_scr, ext_scr, lgate_scr, a_scr, b_scr,
                  u_scr, bu_scr, y_scr, mix_scr, *maybe_xt_scr, first_layer):
    c = pl.program_id(1)
    n_carry = (LRU_CONV - 1) * G

    @pl.when(c == 0)
    def _():
        for b in range(G):
            for hh in range(RET_HEADS):
                scat_scr[hh, :, b * RET_DIM:(b + 1) * RET_DIM] = sret_ref[b, hh]
        lh_ref[...] = slh_ref[...]
        ext_scr[0:n_carry, :] = slc_ref[...]
        s5s_ref[...] = ss5_ref[...]

    if first_layer:
        xt_scr, = maybe_xt_scr
        _time_major(x_ref, xt_scr)

        def x_rows():
            return jnp.concatenate([xt_scr[j] for j in range(D_MODEL // LANES)], axis=1)
    else:
        def x_rows():
            return x_ref[...]

    mnw = mnw_ref[...]
    h_scr[...] = _rms(x_rows(), n1w_ref[...]).astype(BF16)

    def project(col0, width):
        return _dot(h_scr[...], win_ref[:, col0:col0 + width])

    def project_heads(j, dst):
        res = project(j * D_RET, D_RET)
        for hh in range(RET_HEADS):
            dst[hh] = res[:, hh * RET_DIM:(hh + 1) * RET_DIM]

    res = project(4 * D_RET, 2 * D_LRU)
    ext_scr[n_carry:n_carry + R, :] = res[:, :D_LRU]
    lgate_scr[...] = res[:, D_LRU:]
    u_scr[...] = project(4 * D_RET + 2 * D_LRU, D_S5)
    project_heads(0, q_scr)
    bu_scr[...] = _dot(u_scr[...].astype(BF16), bmat_ref[...])

    conv = lcb_ref[...] + lcw_ref[0:1, :] * ext_scr[0:R, :]
    for j in range(1, LRU_CONV):
        conv = conv + lcw_ref[j:j + 1, :] * ext_scr[j * G:j * G + R, :]
    carry_rows = ext_scr[R:R + n_carry, :]
    ext_scr[0:n_carry, :] = carry_rows
    lc_ref[...] = carry_rows
    gates = _dot(conv.astype(BF16), wg_ref[...]) + gb_ref[...]
    z = -lam_ref[...]
    softplus = jnp.maximum(z, 0.0) + jnp.log1p(jnp.exp(-jnp.abs(z)))
    log_a = -LRU_C * jax.nn.sigmoid(gates[:, :D_LRU]) * softplus
    a = jnp.exp(log_a)
    a_scr[...] = a
    b_scr[...] = jnp.sqrt(-jnp.tanh(log_a) * (a * a + 1.0)) * (jax.nn.sigmoid(gates[:, D_LRU:]) * conv)

    project_heads(1, k_scr)

    h_t = lh_ref[...]
    for t in range(L):
        rows = slice(t * G, (t + 1) * G)
        h_t = a_scr[rows, :] * h_t + b_scr[rows, :]
        b_scr[rows, :] = h_t
    lh_ref[...] = h_t
    lru_out = _rms(jax.nn.gelu(lgate_scr[...]) * b_scr[...], mnw[:, D_RET:D_RET + D_LRU])
    mix_scr[:, D_RET:D_RET + D_LRU] = lru_out.astype(BF16)

    project_heads(2, v_scr)

    ar = jnp.broadcast_to(ab_ref[0:1, :], (G, S5_N))
    ai = jnp.broadcast_to(ab_ref[1:2, :], (G, S5_N))
    xr = s5s_ref[:, 0:S5_N]
    xi = s5s_ref[:, S5_N:2 * S5_N]
    for t in range(L):
        rows = slice(t * G, (t + 1) * G)
        xr, xi = (ar * xr - ai * xi + bu_scr[rows, 0:S5_N],
                  ar * xi + ai * xr + bu_scr[rows, S5_N:2 * S5_N])
        bu_scr[rows, 0:S5_N] = xr
        bu_scr[rows, S5_N:2 * S5_N] = xi
    s5s_ref[:, 0:S5_N] = xr
    s5s_ref[:, S5_N:2 * S5_N] = xi

    g_scr[...] = project(3 * D_RET, D_RET)

    cosv = cos_ref[...]
    sinv = sin_ref[...]

    def retention_head(hh):
        cols = slice(hh * RET_DIM, (hh + 1) * RET_DIM)
        q = q_scr[hh]
        k = k_scr[hh]
        v = v_scr[hh]
        qs_f32 = (q * cosv + pltpu.roll(q, RET_DIM // 2, 1) * sinv) * qdec_ref[hh]
        qs = qs_f32.astype(BF16)
        ks_f32 = (k * cosv + pltpu.roll(k, RET_DIM // 2, 1) * sinv) * kdec_ref[hh]
        ks = ks_f32.astype(BF16)
        vb = v.astype(BF16)
        inner = []
        for r0, n_keys in ((0, R // 2), (R // 2, R)):
            scores = lax.dot_general(qs[r0:r0 + R // 2], ks[0:n_keys], (((1,), (1,)), ((), ())),
                                     preferred_element_type=F32) * causal_ref[r0:r0 + R // 2, 0:n_keys]
            inner.append(_dot(scores.astype(BF16), vb[0:n_keys]))
        q_scr[hh] = qs_f32
        k_scr[hh] = ks_f32
        for b in range(G):
            rows = pl.ds(b, L, stride=G)
            lanes = slice(b * RET_DIM, (b + 1) * RET_DIM)
            s_old = scat_scr[hh, :, lanes]
            cross_scr[hh, rows, :] = _dot(q_scr[hh, rows, :].astype(BF16), s_old.astype(BF16))
            scat_scr[hh, :, lanes] = _SDEC[hh] * s_old + lax.dot_general(
                k_scr[hh, rows, :].astype(BF16), v_scr[hh, rows, :].astype(BF16),
                (((0,), (0,)), ((), ())), preferred_element_type=F32)
        o = jnp.concatenate(inner, axis=0) * _SINV[hh] + cross_scr[hh]
        o = o * lax.rsqrt(jnp.mean(o * o, axis=-1, keepdims=True) + EPS)
        mix_scr[:, cols] = (o * mnw[:, cols] * jax.nn.silu(g_scr[:, cols])).astype(BF16)

    retention_head(0)
    y_scr[...] = (_dot(bu_scr[:, 0:S5_N].astype(BF16), cre_ref[...])
                  - _dot(bu_scr[:, S5_N:2 * S5_N].astype(BF16), cim_ref[...]) + s5d_ref[...] * u_scr[...])
    retention_head(1)
    zz = jax.nn.gelu(y_scr[...])
    glu = zz * jax.nn.sigmoid(_dot(zz.astype(BF16), gluw_ref[...]) + glub_ref[...])
    y_scr[...] = glu
    mix_scr[:, D_RET + D_LRU:] = _rms(y_scr[...], mnw[:, D_RET + D_LRU:]).astype(BF16)

    def out_project(col0, width):
        return _dot(mix_scr[:, col0:col0 + width], wout_ref[col0:col0 + width, :])

    xo_ref[...] = x_rows() + out_project(D_RET, D_LRU + D_S5)
    xo_ref[...] += out_project(0, D_RET // 2)
    retention_head(2)
    retention_head(3)

    @pl.when(c == pl.num_programs(1) - 1)
    def _():
        for b in range(G):
            for hh in range(RET_HEADS):
                ret_ref[b, hh] = scat_scr[hh, :, b * RET_DIM:(b + 1) * RET_DIM]

    xo_ref[...] += out_project(D_RET // 2, D_RET // 2)


def _const_spec(shape):
    nd = len(shape)
    return pl.BlockSpec(shape, lambda g, c: (0,) * nd)


def _layer_spec(shape, layer):
    nd = len(shape) - 1
    return pl.BlockSpec((None,) + tuple(shape[1:]), lambda g, c: (layer,) + (0,) * nd)


def _state_spec(shape, layer=None):
    nd = len(shape)
    if layer is None:
        return pl.BlockSpec((None,) + shape, lambda g, c: (g,) + (0,) * nd)
    return pl.BlockSpec((None, None) + shape, lambda g, c: (layer, g) + (0,) * nd)


_MIXER_WEIGHTS = ("n1w", "w_in", "mnw", "lcw", "lcb", "wg", "gb", "lam", "ab", "bmat", "cre", "cim", "s5d",
                  "gluw", "glub", "w_out")
_FFN_WEIGHTS = ("n2w", "ffn_up", "fcw", "fcb", "ffn_down")


def _mixer(x, cos2, sin2, st_ret, st_lh, st_lc, st_s5, p, layer):
    first_layer = layer == 0
    if first_layer:
        ng, nc = x.shape[0] // G, x.shape[1] // L
        in_block = (G, L, D_MODEL)
    else:
        ng, nc = x.shape[0], x.shape[1] // R
        in_block = (None, R, D_MODEL)

    x_spec = pl.BlockSpec((None, R, D_MODEL), lambda g, c: (g, c, 0))
    t_spec = pl.BlockSpec((R, LANES), lambda g, c: (c, 0))
    consts = (_CAUSAL, _QDEC, _KDEC)
    weights = tuple(p[n] for n in _MIXER_WEIGHTS)
    state_shapes = ((G, RET_HEADS, RET_DIM, RET_DIM), (G, D_LRU), ((LRU_CONV - 1) * G, D_LRU), (G, 2 * S5_N))
    return pl.pallas_call(
        functools.partial(_mixer_kernel, first_layer=first_layer),
        grid=(ng, nc),
        in_specs=[pl.BlockSpec(in_block, lambda g, c: (g, c, 0)), t_spec, t_spec]
                 + [_state_spec(s, layer) for s in state_shapes]
                 + [_const_spec(a.shape) for a in consts] + [_layer_spec(w.shape, layer) for w in weights],
        out_specs=[x_spec] + [_state_spec(s) for s in state_shapes],
        out_shape=[jax.ShapeDtypeStruct((ng, nc * R, D_MODEL), F32)]
                  + [jax.ShapeDtypeStruct((ng,) + s, F32) for s in state_shapes],
        scratch_shapes=[
            pltpu.VMEM((R, D_MODEL), BF16),
            pltpu.VMEM((RET_HEADS, R, RET_DIM), F32),
            pltpu.VMEM((RET_HEADS, R, RET_DIM), F32),
            pltpu.VMEM((RET_HEADS, R, RET_DIM), F32),
            pltpu.VMEM((RET_HEADS, R, RET_DIM), F32),
            pltpu.VMEM((RET_HEADS, RET_DIM, G * RET_DIM), F32),
            pltpu.VMEM((R, D_RET), F32),
            pltpu.VMEM((R + (LRU_CONV - 1) * G, D_LRU), F32),
            pltpu.VMEM((R, D_LRU), F32),
            pltpu.VMEM((R, D_LRU), F32),
            pltpu.VMEM((R, D_LRU), F32),
            pltpu.VMEM((R, D_S5), F32),
            pltpu.VMEM((R, 2 * S5_N), F32),
            pltpu.VMEM((R, D_S5), F32),
            pltpu.VMEM((R, D_MODEL), BF16),
        ] + ([pltpu.VMEM((D_MODEL // LANES, R, LANES), F32)] if first_layer else []),
        compiler_params=pltpu.CompilerParams(
            dimension_semantics=("arbitrary", "arbitrary"), vmem_limit_bytes=VMEM_LIMIT),
        name="mixer",
    )(x, cos2, sin2, st_ret, st_lh, st_lc, st_s5, *consts, *weights)


def _ffn_kernel(x_ref, sfc_ref, n2w_ref, up_ref, cw_ref, cb_ref, down_ref, fnw_ref,
                xo_ref, fc_ref, h_scr, extg_scr, extv_scr, *maybe_y_scr, last_layer):
    c = pl.program_id(1)
    n_carry = (FFN_CONV - 1) * G
    rows = x_ref.shape[0]

    @pl.when(c == 0)
    def _():
        fc_ref[...] = sfc_ref[...]

    h_scr[...] = _rms(x_ref[...], n2w_ref[...]).astype(BF16)
    acc = x_ref[...]
    col0 = 0
    for width in FF_TILES:
        halves = [(ext, slice(half * D_FF + col0, half * D_FF + col0 + width))
                  for half, ext in enumerate((extg_scr, extv_scr))]
        for ext, cols in halves:
            ext[0:n_carry, 0:width] = fc_ref[:, cols]
            ext[n_carry:n_carry + rows, 0:width] = _dot(h_scr[...], up_ref[:, cols])
            fc_ref[:, cols] = ext[rows:rows + n_carry, 0:width]
        conv = []
        for ext, cols in halves:
            y = cb_ref[:, cols] + cw_ref[0:1, cols] * ext[0:rows, 0:width]
            for k in range(1, FFN_CONV):
                y = y + cw_ref[k:k + 1, cols] * ext[k * G:k * G + rows, 0:width]
            conv.append(y)
        act = (jax.nn.silu(conv[0]) * conv[1]).astype(BF16)
        acc = acc + _dot(act, down_ref[col0:col0 + width, :])
        col0 += width
    if not last_layer:
        xo_ref[...] = acc
        return
    y_scr, = maybe_y_scr
    acc = _rms(acc, fnw_ref[...])
    for j in range(D_MODEL // LANES):
        y_scr[j] = acc[:, j * LANES:(j + 1) * LANES]
    steps = rows // G
    for b in range(G):
        for j in range(D_MODEL // LANES):
            xo_ref[b, :, j * LANES:(j + 1) * LANES] = y_scr[j, pl.ds(b, steps, stride=G), :]


def _ffn(xg, st_fc, p, fnw, layer, batch):
    ng, total_rows, _ = xg.shape
    steps = min(L_FFN, total_rows // G)
    rows = steps * G
    nc = total_rows // rows
    last_layer = layer == DEPTH - 1
    x_spec = pl.BlockSpec((None, rows, D_MODEL), lambda g, c: (g, c, 0))
    fc_shape = ((FFN_CONV - 1) * G, 2 * D_FF)
    weights = tuple(p[n] for n in _FFN_WEIGHTS)
    scratch = [
        pltpu.VMEM((rows, D_MODEL), BF16),
        pltpu.VMEM((rows + (FFN_CONV - 1) * G, max(FF_TILES)), F32),
        pltpu.VMEM((rows + (FFN_CONV - 1) * G, max(FF_TILES)), F32),
    ]
    if last_layer:
        out_x_spec = pl.BlockSpec((G, steps, D_MODEL), lambda g, c: (g, c, 0))
        out_x_shape = jax.ShapeDtypeStruct((batch, total_rows // G, D_MODEL), F32)
        scratch.append(pltpu.VMEM((D_MODEL // LANES, rows, LANES), F32))
    else:
        out_x_spec, out_x_shape = x_spec, jax.ShapeDtypeStruct(xg.shape, F32)
    return pl.pallas_call(
        functools.partial(_ffn_kernel, last_layer=last_layer),
        grid=(ng, nc),
        in_specs=[x_spec, _state_spec(fc_shape, layer)]
                 + [_layer_spec(w.shape, layer) for w in weights] + [_const_spec(fnw.shape)],
        out_specs=[out_x_spec, _state_spec(fc_shape)],
        out_shape=[out_x_shape, jax.ShapeDtypeStruct((ng,) + fc_shape, F32)],
        scratch_shapes=scratch,
        compiler_params=pltpu.CompilerParams(
            dimension_semantics=("arbitrary", "arbitrary"), vmem_limit_bytes=VMEM_LIMIT),
        name="convffn",
    )(xg, st_fc, *weights, fnw)


def _prepare_params(norm1_w, w_in, mix_norm_w, lru_conv_w, lru_conv_b, lru_gate_w, lru_gate_b, lru_lambda,
                    s5_lambda_re, s5_lambda_im, s5_log_dt, s5_B, s5_C, s5_D, s5_glu_w, s5_glu_b, w_out, norm2_w,
                    ffn_up, ffn_conv_w, ffn_conv_b, ffn_down):
    s5_ab, s5_bbre, s5_bbim = _s5_discretize(s5_lambda_re, s5_lambda_im, s5_log_dt, s5_B)
    vec = lambda a: a.reshape(DEPTH, 1, -1)
    return {
        "n1w": vec(norm1_w), "w_in": w_in.astype(BF16), "mnw": vec(mix_norm_w),
        "lcw": lru_conv_w, "lcb": vec(lru_conv_b),
        "wg": _lru_gate_matrix(lru_gate_w).astype(BF16), "gb": vec(lru_gate_b),
        "lam": vec(lru_lambda), "ab": s5_ab,
        "bmat": jnp.concatenate([_s5_in_matrix(s5_bbre), _s5_in_matrix(s5_bbim)], axis=2).astype(BF16),
        "cre": _s5_out_matrix(s5_C[..., 0]).astype(BF16), "cim": _s5_out_matrix(s5_C[..., 1]).astype(BF16),
        "s5d": vec(s5_D), "gluw": s5_glu_w.astype(BF16), "glub": vec(s5_glu_b),
        "w_out": w_out.astype(BF16), "n2w": vec(norm2_w), "ffn_up": ffn_up.astype(BF16),
        "fcw": ffn_conv_w, "fcb": vec(ffn_conv_b), "ffn_down": ffn_down.astype(BF16),
    }


def _trunk(x, pos0, states, p, fnw):
    b, t, _ = x.shape
    ng = b // G
    xg = x
    cos2, sin2 =(jnp.repeat(tab, G, axis=0) for tab in _rope_tables(t, pos0))
    st_ret, st_lh, st_lc, st_s5, st_fc = states
    new = []
    for layer in range(DEPTH):
        xg, ret, lh, lc, s5 = _mixer(xg, cos2, sin2, st_ret, st_lh, st_lc, st_s5, p, layer)
        xg, fc = _ffn(xg, st_fc, p, fnw, layer, b)
        new.append((ret, lh, lc, s5, fc))
    ret, lh, lc, s5, fc = (jnp.stack([n[i] for n in new]) for i in range(5))
    ret = ret.reshape(DEPTH, b, RET_HEADS, RET_DIM, RET_DIM)
    lh = lh.reshape(DEPTH, b, D_LRU)
    lc = lc.reshape(DEPTH, ng, LRU_CONV - 1, G, D_LRU).transpose(0, 1, 3, 2, 4).reshape(DEPTH, b, LRU_CONV - 1, D_LRU)
    s5 = s5.reshape(DEPTH, b, 2, S5_GROUPS, S5_STATE).transpose(0, 1, 3, 4, 2)
    fc = fc.reshape(DEPTH, ng, FFN_CONV - 1, G, 2 * D_FF).transpose(0, 1, 3, 2, 4).reshape(DEPTH, b, FFN_CONV - 1, 2 * D_FF)
    return xg, (ret, lh, lc, s5, fc)


def _states_to_groups(state_ret, state_lru_h, state_lru_conv, state_s5, state_ffn_conv):
    b = state_ret.shape[1]
    ng = b // G
    ret = state_ret.reshape(DEPTH, ng, G, RET_HEADS, RET_DIM, RET_DIM)
    lh = state_lru_h.reshape(DEPTH, ng, G, D_LRU)
    lc = state_lru_conv.reshape(DEPTH, ng, G, LRU_CONV - 1, D_LRU).transpose(0, 1, 3, 2, 4).reshape(
        DEPTH, ng, (LRU_CONV - 1) * G, D_LRU)
    s5 = state_s5.reshape(DEPTH, ng, G, S5_N, 2).transpose(0, 1, 2, 4, 3).reshape(DEPTH, ng, G, 2 * S5_N)
    fc = state_ffn_conv.reshape(DEPTH, ng, G, FFN_CONV - 1, 2 * D_FF).transpose(0, 1, 3, 2, 4).reshape(
        DEPTH, ng, (FFN_CONV - 1) * G, 2 * D_FF)
    return ret, lh, lc, s5, fc


def kernel(x_prompt, x_sample, state_ret, state_lru_h, state_lru_conv, state_s5, state_ffn_conv, norm1_w, w_in, mix_norm_w, lru_conv_w, lru_conv_b, lru_gate_w, lru_gate_b, lru_lambda, s5_lambda_re, s5_lambda_im, s5_log_dt, s5_B, s5_C, s5_D, s5_glu_w, s5_glu_b, w_out, norm2_w, ffn_up, ffn_conv_w, ffn_conv_b, ffn_down, final_norm_w):
    p = _prepare_params(norm1_w, w_in, mix_norm_w, lru_conv_w, lru_conv_b, lru_gate_w, lru_gate_b, lru_lambda,
                        s5_lambda_re, s5_lambda_im, s5_log_dt, s5_B, s5_C, s5_D, s5_glu_w, s5_glu_b, w_out, norm2_w,
                        ffn_up, ffn_conv_w, ffn_conv_b, ffn_down)
    fnw = final_norm_w[None]

    bp = x_prompt.shape[0]
    zero_states = _states_to_groups(
        jnp.zeros((DEPTH, bp, RET_HEADS, RET_DIM, RET_DIM), F32), jnp.zeros((DEPTH, bp, D_LRU), F32),
        jnp.zeros((DEPTH, bp, LRU_CONV - 1, D_LRU), F32), jnp.zeros((DEPTH, bp, S5_GROUPS, S5_STATE, 2), F32),
        jnp.zeros((DEPTH, bp, FFN_CONV - 1, 2 * D_FF), F32))
    y_prompt, new_p = _trunk(x_prompt, 0, zero_states, p, fnw)
    sample_states = _states_to_groups(state_ret, state_lru_h, state_lru_conv, state_s5, state_ffn_conv)
    y_sample, new_s = _trunk(x_sample, PAST_LEN, sample_states, p, fnw)
    return (y_prompt, y_sample) + new_p + new_s
```

```python
import functools
import math

import numpy as np
import jax
import jax.numpy as jnp
from jax import lax
from jax.experimental import pallas as pl
from jax.experimental.pallas import tpu as pltpu

D_MODEL = 1024
DEPTH = 2
PAST_LEN = 2048
RET_HEADS = 4
RET_DIM = 128
D_RET = RET_HEADS * RET_DIM
D_LRU = 256
LRU_BLOCKS = 4
LRU_BLOCK = D_LRU // LRU_BLOCKS
LRU_CONV = 4
LRU_C = 8.0
D_S5 = 256
S5_GROUP = 16
S5_GROUPS = D_S5 // S5_GROUP
S5_STATE = 64
S5_N = S5_GROUPS * S5_STATE
D_FF = 2816
FFN_CONV = 3
ROPE_BASE = 10000.0
EPS = 1e-6
D_IN = 4 * D_RET + 2 * D_LRU + D_S5

G = 8
LANES = 128
L = 64
R = L * G
MXU_TILE = 256
L_FFN = 128
FF_TILES = (4 * MXU_TILE, 4 * MXU_TILE, 3 * MXU_TILE)
assert sum(FF_TILES) == D_FF
VMEM_LIMIT = 56 * 1024 * 1024

F32 = jnp.float32
BF16 = jnp.bfloat16

_LOG_G = [math.log1p(-(2.0 ** (-5.0 - h))) for h in range(RET_HEADS)]


def _ret_tables():
    t = (np.arange(R) // G).astype(np.float64)
    b = np.arange(R) % G
    lg = np.asarray(_LOG_G)[:, None]
    qdec = np.exp(lg * (t + 1.0))[:, :, None] * np.ones((1, 1, RET_DIM))
    kdec = np.exp(lg * (L - 1.0 - t))[:, :, None] * np.ones((1, 1, RET_DIM)) * RET_DIM ** -0.5
    causal = (b[:, None] == b[None, :]) & (t[:, None] >= t[None, :])
    return causal.astype(np.float32), qdec.astype(np.float32), kdec.astype(np.float32)


_CAUSAL, _QDEC, _KDEC = _ret_tables()
_SDEC = [math.exp(lg * L) for lg in _LOG_G]
_SINV = [math.exp(-lg * L) for lg in _LOG_G]


def _rms(x, w):
    return x * lax.rsqrt(jnp.mean(x * x, axis=-1, keepdims=True) + EPS) * w


def _dot(a, b):
    return jnp.dot(a, b, preferred_element_type=F32)


def _rope_kernel(cos_ref, sin_ref, *, pos0, rows):
    base = pl.program_id(0) * rows + pos0
    lane = lax.broadcasted_iota(jnp.int32, (rows, LANES), 1)
    row = lax.broadcasted_iota(jnp.int32, (rows, LANES), 0)
    half = RET_DIM // 2
    idx = jnp.where(lane >= half, lane - half, lane).astype(F32)
    freq = jnp.exp(idx * (-math.log(ROPE_BASE) / half))
    ang = (row + base).astype(F32) * freq
    s = jnp.sin(ang)
    cos_ref[...] = jnp.cos(ang)
    sin_ref[...] = jnp.where(lane >= half, s, -s)


def _rope_tables(T, pos0):
    rows = min(T, 256)
    return pl.pallas_call(
        functools.partial(_rope_kernel, pos0=pos0, rows=rows),
        grid=(T // rows,),
        out_specs=[pl.BlockSpec((rows, LANES), lambda i: (i, 0))] * 2,
        out_shape=[jax.ShapeDtypeStruct((T, LANES), F32)] * 2,
        name="rope_tables",
    )()


def _s5_prep_kernel(lre_ref, lim_ref, ldt_ref, bre_ref, bim_ref, ab_ref, bbre_ref, bbim_ref):
    dt = jnp.exp(ldt_ref[...])
    lre = lre_ref[...]
    lim = lim_ref[...]
    mag = jnp.exp(lre * dt)
    ph = lim * dt
    abr = mag * jnp.cos(ph)
    abi = mag * jnp.sin(ph)
    den = lre * lre + lim * lim
    nr = abr - 1.0
    fre = (nr * lre + abi * lim) / den
    fim = (abi * lre - nr * lim) / den
    ab_ref[0:1, :] = abr
    ab_ref[1:2, :] = abi
    bre = bre_ref[...]
    bim = bim_ref[...]
    bbre_ref[...] = fre * bre - fim * bim
    bbim_ref[...] = fre * bim + fim * bre


def _s5_discretize(lam_re, lam_im, log_dt, s5_B):
    lre = lam_re.reshape(DEPTH, 1, S5_N)
    lim = lam_im.reshape(DEPTH, 1, S5_N)
    ldt = jnp.broadcast_to(log_dt[:, :, None], (DEPTH, S5_GROUPS, S5_STATE)).reshape(DEPTH, 1, S5_N)
    bt = s5_B.transpose(0, 3, 1, 2, 4).reshape(DEPTH, S5_GROUP, S5_N, 2)
    row = lambda r: pl.BlockSpec((None, r, S5_N), lambda d: (d, 0, 0))
    return pl.pallas_call(
        _s5_prep_kernel,
        grid=(DEPTH,),
        in_specs=[row(1), row(1), row(1), row(S5_GROUP), row(S5_GROUP)],
        out_specs=[row(2), row(S5_GROUP), row(S5_GROUP)],
        out_shape=[jax.ShapeDtypeStruct((DEPTH, 2, S5_N), F32),
                   jax.ShapeDtypeStruct((DEPTH, S5_GROUP, S5_N), F32),
                   jax.ShapeDtypeStruct((DEPTH, S5_GROUP, S5_N), F32)],
        name="s5_discretize",
    )(lre, lim, ldt, bt[..., 0], bt[..., 1])


def _s5_in_matrix(bb):
    gi = jnp.arange(S5_GROUPS)
    same = gi[:, None, None, None] == gi[None, None, :, None]
    full = jnp.where(same, bb.reshape(DEPTH, 1, S5_GROUP, S5_GROUPS, S5_STATE), 0.0)
    return full.reshape(DEPTH, D_S5, S5_N)


def _s5_out_matrix(c):
    gi = jnp.arange(S5_GROUPS)
    same = gi[:, None, None, None] == gi[None, None, :, None]
    full = jnp.where(same, c.transpose(0, 3, 1, 2)[:, None], 0.0)
    return full.reshape(DEPTH, S5_N, D_S5)


def _lru_gate_matrix(gw):
    bi = jnp.arange(LRU_BLOCKS)
    same = bi[:, None, None, None, None] == bi[None, None, None, :, None]
    wt = gw.transpose(0, 2, 3, 1, 4)[:, :, :, :, None, :]
    return jnp.where(same, wt, 0.0).reshape(DEPTH, D_LRU, 2 * D_LRU)


def _time_major(x3_ref, slab_scr):
    for b in range(G):
        for j in range(D_MODEL // LANES):
            slab_scr[j, pl.ds(b, L, stride=G), :] = x3_ref[b, :, j * LANES:(j + 1) * LANES]


def _mixer_kernel(x_ref, cos_ref, sin_ref, sret_ref, slh_ref, slc_ref, ss5_ref,
                  causal_ref, qdec_ref, kdec_ref,
                  n1w_ref, win_ref, mnw_ref, lcw_ref, lcb_ref, wg_ref, gb_ref, lam_ref,
                  ab_ref, bmat_ref, cre_ref, cim_ref, s5d_ref, gluw_ref, glub_ref, wout_ref,
                  xo_ref, ret_ref, lh_ref, lc_ref, s5s_ref,
                  h_scr, q_scr, k_scr, v_scr, cross_scr, scat_scr, g_scr, ext_scr, lgate_scr, a_scr, b_scr,
                  u_scr, bu_scr, y_scr, mix_scr, *maybe_xt_scr, first_layer):
    c = pl.program_id(1)
    n_carry = (LRU_CONV - 1) * G

    @pl.when(c == 0)
    def _():
        for b in range(G):
            for hh in range(RET_HEADS):
                scat_scr[hh, :, b * RET_DIM:(b + 1) * RET_DIM] = sret_ref[b, hh]
        lh_ref[...] = slh_ref[...]
        ext_scr[0:n_carry, :] = slc_ref[...]
        s5s_ref[...] = ss5_ref[...]

    if first_layer:
        xt_scr, = maybe_xt_scr
        _time_major(x_ref, xt_scr)

        def x_rows():
            return jnp.concatenate([xt_scr[j] for j in range(D_MODEL // LANES)], axis=1)
    else:
        def x_rows():
            return x_ref[...]

    mnw = mnw_ref[...]
    h_scr[...] = _rms(x_rows(), n1w_ref[...]).astype(BF16)

    def project(col0, width):
        return _dot(h_scr[...], win_ref[:, col0:col0 + width])

    def project_heads(j, dst):
        res = project(j * D_RET, D_RET)
        for hh in range(RET_HEADS):
            dst[hh] = res[:, hh * RET_DIM:(hh + 1) * RET_DIM]

    res = project(4 * D_RET, 2 * D_LRU)
    ext_scr[n_carry:n_carry + R, :] = res[:, :D_LRU]
    lgate_scr[...] = res[:, D_LRU:]
    u_scr[...] = project(4 * D_RET + 2 * D_LRU, D_S5)
    project_heads(0, q_scr)
    bu_scr[...] = _dot(u_scr[...].astype(BF16), bmat_ref[...])

    conv = lcb_ref[...] + lcw_ref[0:1, :] * ext_scr[0:R, :]
    for j in range(1, LRU_CONV):
        conv = conv + lcw_ref[j:j + 1, :] * ext_scr[j * G:j * G + R, :]
    carry_rows = ext_scr[R:R + n_carry, :]
    ext_scr[0:n_carry, :] = carry_rows
    lc_ref[...] = carry_rows
    gates = _dot(conv.astype(BF16), wg_ref[...]) + gb_ref[...]
    z = -lam_ref[...]
    softplus = jnp.maximum(z, 0.0) + jnp.log1p(jnp.exp(-jnp.abs(z)))
    log_a = -LRU_C * jax.nn.sigmoid(gates[:, :D_LRU]) * softplus
    a = jnp.exp(log_a)
    a_scr[...] = a
    b_scr[...] = jnp.sqrt(-jnp.tanh(log_a) * (a * a + 1.0)) * (jax.nn.sigmoid(gates[:, D_LRU:]) * conv)

    project_heads(1, k_scr)

    h_t = lh_ref[...]
    for t in range(L):
        rows = slice(t * G, (t + 1) * G)
        h_t = a_scr[rows, :] * h_t + b_scr[rows, :]
        b_scr[rows, :] = h_t
    lh_ref[...] = h_t
    lru_out = _rms(jax.nn.gelu(lgate_scr[...]) * b_scr[...], mnw[:, D_RET:D_RET + D_LRU])
    mix_scr[:, D_RET:D_RET + D_LRU] = lru_out.astype(BF16)

    project_heads(2, v_scr)

    ar = jnp.broadcast_to(ab_ref[0:1, :], (G, S5_N))
    ai = jnp.broadcast_to(ab_ref[1:2, :], (G, S5_N))
    xr = s5s_ref[:, 0:S5_N]
    xi = s5s_ref[:, S5_N:2 * S5_N]
    for t in range(L):
        rows = slice(t * G, (t + 1) * G)
        xr, xi = (ar * xr - ai * xi + bu_scr[rows, 0:S5_N],
                  ar * xi + ai * xr + bu_scr[rows, S5_N:2 * S5_N])
        bu_scr[rows, 0:S5_N] = xr
        bu_scr[rows, S5_N:2 * S5_N] = xi
    s5s_ref[:, 0:S5_N] = xr
    s5s_ref[:, S5_N:2 * S5_N] = xi

    g_scr[...] = project(3 * D_RET, D_RET)

    t0 = pl.multiple_of(c * L, L)
    cosv = jnp.repeat(cos_ref[pl.ds(t0, L), :], G, axis=0)
    sinv = jnp.repeat(sin_ref[pl.ds(t0, L), :], G, axis=0)

    def retention_head(hh):
        cols = slice(hh * RET_DIM, (hh + 1) * RET_DIM)
        q = q_scr[hh]
        k = k_scr[hh]
        v = v_scr[hh]
        qs_f32 = (q * cosv + pltpu.roll(q, RET_DIM // 2, 1) * sinv) * qdec_ref[hh]
        qs = qs_f32.astype(BF16)
        ks_f32 = (k * cosv + pltpu.roll(k, RET_DIM // 2, 1) * sinv) * kdec_ref[hh]
        ks = ks_f32.astype(BF16)
        vb = v.astype(BF16)
        inner = []
        for r0, n_keys in ((0, R // 2), (R // 2, R)):
            scores = lax.dot_general(qs[r0:r0 + R // 2], ks[0:n_keys], (((1,), (1,)), ((), ())),
                                     preferred_element_type=F32) * causal_ref[r0:r0 + R // 2, 0:n_keys]
            inner.append(_dot(scores.astype(BF16), vb[0:n_keys]))
        q_scr[hh] = qs_f32
        k_scr[hh] = ks_f32
        for b in range(G):
            rows = pl.ds(b, L, stride=G)
            lanes = slice(b * RET_DIM, (b + 1) * RET_DIM)
            s_old = scat_scr[hh, :, lanes]
            cross_scr[hh, rows, :] = _dot(q_scr[hh, rows, :].astype(BF16), s_old.astype(BF16))
            scat_scr[hh, :, lanes] = _SDEC[hh] * s_old + lax.dot_general(
                k_scr[hh, rows, :].astype(BF16), v_scr[hh, rows, :].astype(BF16),
                (((0,), (0,)), ((), ())), preferred_element_type=F32)
        o = jnp.concatenate(inner, axis=0) * _SINV[hh] + cross_scr[hh]
        o = o * lax.rsqrt(jnp.mean(o * o, axis=-1, keepdims=True) + EPS)
        mix_scr[:, cols] = (o * mnw[:, cols] * jax.nn.silu(g_scr[:, cols])).astype(BF16)

    retention_head(0)
    y_scr[...] = (_dot(bu_scr[:, 0:S5_N].astype(BF16), cre_ref[...])
                  - _dot(bu_scr[:, S5_N:2 * S5_N].astype(BF16), cim_ref[...]) + s5d_ref[...] * u_scr[...])
    retention_head(1)
    zz = jax.nn.gelu(y_scr[...])
    glu = zz * jax.nn.sigmoid(_dot(zz.astype(BF16), gluw_ref[...]) + glub_ref[...])
    y_scr[...] = glu
    mix_scr[:, D_RET + D_LRU:] = _rms(y_scr[...], mnw[:, D_RET + D_LRU:]).astype(BF16)

    def out_project(col0, width):
        return _dot(mix_scr[:, col0:col0 + width], wout_ref[col0:col0 + width, :])

    xo_ref[...] = x_rows() + out_project(D_RET, D_LRU + D_S5)
    xo_ref[...] += out_project(0, D_RET // 2)
    retention_head(2)
    retention_head(3)

    @pl.when(c == pl.num_programs(1) - 1)
    def _():
        for b in range(G):
            for hh in range(RET_HEADS):
                ret_ref[b, hh] = scat_scr[hh, :, b * RET_DIM:(b + 1) * RET_DIM]

    xo_ref[...] += out_project(D_RET // 2, D_RET // 2)


def _const_spec(shape):
    nd = len(shape)
    return pl.BlockSpec(shape, lambda g, c: (0,) * nd)


def _layer_spec(shape, layer):
    nd = len(shape) - 1
    return pl.BlockSpec((None,) + tuple(shape[1:]), lambda g, c: (layer,) + (0,) * nd)


def _state_spec(shape, layer=None):
    nd = len(shape)
    if layer is None:
        return pl.BlockSpec((None,) + shape, lambda g, c: (g,) + (0,) * nd)
    return pl.BlockSpec((None, None) + shape, lambda g, c: (layer, g) + (0,) * nd)


_MIXER_WEIGHTS = ("n1w", "w_in", "mnw", "lcw", "lcb", "wg", "gb", "lam", "ab", "bmat", "cre", "cim", "s5d",
                  "gluw", "glub", "w_out")
_FFN_WEIGHTS = ("n2w", "ffn_up", "fcw", "fcb", "ffn_down")


def _mixer(x, cos2, sin2, st_ret, st_lh, st_lc, st_s5, p, layer):
    first_layer = layer == 0
    if first_layer:
        ng, nc = x.shape[0] // G, x.shape[1] // L
        in_block = (G, L, D_MODEL)
    else:
        ng, nc = x.shape[0], x.shape[1] // R
        in_block = (None, R, D_MODEL)

    x_spec = pl.BlockSpec((None, R, D_MODEL), lambda g, c: (g, c, 0))
    t_spec = _const_spec(cos2.shape)
    consts = (_CAUSAL, _QDEC, _KDEC)
    weights = tuple(p[n] for n in _MIXER_WEIGHTS)
    state_shapes = ((G, RET_HEADS, RET_DIM, RET_DIM), (G, D_LRU), ((LRU_CONV - 1) * G, D_LRU), (G, 2 * S5_N))
    return pl.pallas_call(
        functools.partial(_mixer_kernel, first_layer=first_layer),
        grid=(ng, nc),
        in_specs=[pl.BlockSpec(in_block, lambda g, c: (g, c, 0)), t_spec, t_spec]
                 + [_state_spec(s, layer) for s in state_shapes]
                 + [_const_spec(a.shape) for a in consts] + [_layer_spec(w.shape, layer) for w in weights],
        out_specs=[x_spec] + [_state_spec(s) for s in state_shapes],
        out_shape=[jax.ShapeDtypeStruct((ng, nc * R, D_MODEL), F32)]
                  + [jax.ShapeDtypeStruct((ng,) + s, F32) for s in state_shapes],
        scratch_shapes=[
            pltpu.VMEM((R, D_MODEL), BF16),
            pltpu.VMEM((RET_HEADS, R, RET_DIM), F32),
            pltpu.VMEM((RET_HEADS, R, RET_DIM), F32),
            pltpu.VMEM((RET_HEADS, R, RET_DIM), F32),
            pltpu.VMEM((RET_HEADS, R, RET_DIM), F32),
            pltpu.VMEM((RET_HEADS, RET_DIM, G * RET_DIM), F32),
            pltpu.VMEM((R, D_RET), F32),
            pltpu.VMEM((R + (LRU_CONV - 1) * G, D_LRU), F32),
            pltpu.VMEM((R, D_LRU), F32),
            pltpu.VMEM((R, D_LRU), F32),
            pltpu.VMEM((R, D_LRU), F32),
            pltpu.VMEM((R, D_S5), F32),
            pltpu.VMEM((R, 2 * S5_N), F32),
            pltpu.VMEM((R, D_S5), F32),
            pltpu.VMEM((R, D_MODEL), BF16),
        ] + ([pltpu.VMEM((D_MODEL // LANES, R, LANES), F32)] if first_layer else []),
        compiler_params=pltpu.CompilerParams(
            dimension_semantics=("arbitrary", "arbitrary"), vmem_limit_bytes=VMEM_LIMIT),
        name="mixer",
    )(x, cos2, sin2, st_ret, st_lh, st_lc, st_s5, *consts, *weights)


def _ffn_kernel(x_ref, sfc_ref, n2w_ref, up_ref, cw_ref, cb_ref, down_ref, fnw_ref,
                xo_ref, fc_ref, h_scr, extg_scr, extv_scr, *maybe_y_scr, last_layer):
    c = pl.program_id(1)
    n_carry = (FFN_CONV - 1) * G
    rows = x_ref.shape[0]

    @pl.when(c == 0)
    def _():
        fc_ref[...] = sfc_ref[...]

    h_scr[...] = _rms(x_ref[...], n2w_ref[...]).astype(BF16)
    acc = x_ref[...]
    col0 = 0
    for width in FF_TILES:
        halves = [(ext, slice(half * D_FF + col0, half * D_FF + col0 + width))
                  for half, ext in enumerate((extg_scr, extv_scr))]
        for ext, cols in halves:
            ext[0:n_carry, 0:width] = fc_ref[:, cols]
            ext[n_carry:n_carry + rows, 0:width] = _dot(h_scr[...], up_ref[:, cols])
            fc_ref[:, cols] = ext[rows:rows + n_carry, 0:width]
        conv = []
        for ext, cols in halves:
            y = cb_ref[:, cols] + cw_ref[0:1, cols] * ext[0:rows, 0:width]
            for k in range(1, FFN_CONV):
                y = y + cw_ref[k:k + 1, cols] * ext[k * G:k * G + rows, 0:width]
            conv.append(y)
        act = (jax.nn.silu(conv[0]) * conv[1]).astype(BF16)
        acc = acc + _dot(act, down_ref[col0:col0 + width, :])
        col0 += width
    if not last_layer:
        xo_ref[...] = acc
        return
    y_scr, = maybe_y_scr
    acc = _rms(acc, fnw_ref[...])
    for j in range(D_MODEL // LANES):
        y_scr[j] = acc[:, j * LANES:(j + 1) * LANES]
    steps = rows // G
    for b in range(G):
        for j in range(D_MODEL // LANES):
            xo_ref[b, :, j * LANES:(j + 1) * LANES] = y_scr[j, pl.ds(b, steps, stride=G), :]


def _ffn(xg, st_fc, p, fnw, layer, batch):
    ng, total_rows, _ = xg.shape
    steps = min(L_FFN, total_rows // G)
    rows = steps * G
    nc = total_rows // rows
    last_layer = layer == DEPTH - 1
    x_spec = pl.BlockSpec((None, rows, D_MODEL), lambda g, c: (g, c, 0))
    fc_shape = ((FFN_CONV - 1) * G, 2 * D_FF)
    weights = tuple(p[n] for n in _FFN_WEIGHTS)
    scratch = [
        pltpu.VMEM((rows, D_MODEL), BF16),
        pltpu.VMEM((rows + (FFN_CONV - 1) * G, max(FF_TILES)), F32),
        pltpu.VMEM((rows + (FFN_CONV - 1) * G, max(FF_TILES)), F32),
    ]
    if last_layer:
        out_x_spec = pl.BlockSpec((G, steps, D_MODEL), lambda g, c: (g, c, 0))
        out_x_shape = jax.ShapeDtypeStruct((batch, total_rows // G, D_MODEL), F32)
        scratch.append(pltpu.VMEM((D_MODEL // LANES, rows, LANES), F32))
    else:
        out_x_spec, out_x_shape = x_spec, jax.ShapeDtypeStruct(xg.shape, F32)
    return pl.pallas_call(
        functools.partial(_ffn_kernel, last_layer=last_layer),
        grid=(ng, nc),
        in_specs=[x_spec, _state_spec(fc_shape, layer)]
                 + [_layer_spec(w.shape, layer) for w in weights] + [_const_spec(fnw.shape)],
        out_specs=[out_x_spec, _state_spec(fc_shape)],
        out_shape=[out_x_shape, jax.ShapeDtypeStruct((ng,) + fc_shape, F32)],
        scratch_shapes=scratch,
        compiler_params=pltpu.CompilerParams(
            dimension_semantics=("arbitrary", "arbitrary"), vmem_limit_bytes=VMEM_LIMIT),
        name="convffn",
    )(xg, st_fc, *weights, fnw)


def _prepare_params(norm1_w, w_in, mix_norm_w, lru_conv_w, lru_conv_b, lru_gate_w, lru_gate_b, lru_lambda,
                    s5_lambda_re, s5_lambda_im, s5_log_dt, s5_B, s5_C, s5_D, s5_glu_w, s5_glu_b, w_out, norm2_w,
                    ffn_up, ffn_conv_w, ffn_conv_b, ffn_down):
    s5_ab, s5_bbre, s5_bbim = _s5_discretize(s5_lambda_re, s5_lambda_im, s5_log_dt, s5_B)
    vec = lambda a: a.reshape(DEPTH, 1, -1)
    return {
        "n1w": vec(norm1_w), "w_in": w_in.astype(BF16), "mnw": vec(mix_norm_w),
        "lcw": lru_conv_w, "lcb": vec(lru_conv_b),
        "wg": _lru_gate_matrix(lru_gate_w).astype(BF16), "gb": vec(lru_gate_b),
        "lam": vec(lru_lambda), "ab": s5_ab,
        "bmat": jnp.concatenate([_s5_in_matrix(s5_bbre), _s5_in_matrix(s5_bbim)], axis=2).astype(BF16),
        "cre": _s5_out_matrix(s5_C[..., 0]).astype(BF16), "cim": _s5_out_matrix(s5_C[..., 1]).astype(BF16),
        "s5d": vec(s5_D), "gluw": s5_glu_w.astype(BF16), "glub": vec(s5_glu_b),
        "w_out": w_out.astype(BF16), "n2w": vec(norm2_w), "ffn_up": ffn_up.astype(BF16),
        "fcw": ffn_conv_w, "fcb": vec(ffn_conv_b), "ffn_down": ffn_down.astype(BF16),
    }


def _trunk(x, pos0, states, p, fnw):
    b, t, _ = x.shape
    ng = b // G
    xg = x
    assert b % G == 0 and t % L == 0 and (t % L_FFN == 0 or t < L_FFN), (b, t)
    cos2, sin2 = _rope_tables(t, pos0)
    st_ret, st_lh, st_lc, st_s5, st_fc = states
    new = []
    for layer in range(DEPTH):
        xg, ret, lh, lc, s5 = _mixer(xg, cos2, sin2, st_ret, st_lh, st_lc, st_s5, p, layer)
        xg, fc = _ffn(xg, st_fc, p, fnw, layer, b)
        new.append((ret, lh, lc, s5, fc))
    ret, lh, lc, s5, fc = (jnp.stack([n[i] for n in new]) for i in range(5))
    ret = ret.reshape(DEPTH, b, RET_HEADS, RET_DIM, RET_DIM)
    lh = lh.reshape(DEPTH, b, D_LRU)
    lc = lc.reshape(DEPTH, ng, LRU_CONV - 1, G, D_LRU).transpose(0, 1, 3, 2, 4).reshape(DEPTH, b, LRU_CONV - 1, D_LRU)
    s5 = s5.reshape(DEPTH, b, 2, S5_GROUPS, S5_STATE).transpose(0, 1, 3, 4, 2)
    fc = fc.reshape(DEPTH, ng, FFN_CONV - 1, G, 2 * D_FF).transpose(0, 1, 3, 2, 4).reshape(DEPTH, b, FFN_CONV - 1, 2 * D_FF)
    return xg, (ret, lh, lc, s5, fc)


def _states_to_groups(state_ret, state_lru_h, state_lru_conv, state_s5, state_ffn_conv):
    b = state_ret.shape[1]
    ng = b // G
    ret = state_ret.reshape(DEPTH, ng, G, RET_HEADS, RET_DIM, RET_DIM)
    lh = state_lru_h.reshape(DEPTH, ng, G, D_LRU)
    lc = state_lru_conv.reshape(DEPTH, ng, G, LRU_CONV - 1, D_LRU).transpose(0, 1, 3, 2, 4).reshape(
        DEPTH, ng, (LRU_CONV - 1) * G, D_LRU)
    s5 = state_s5.reshape(DEPTH, ng, G, S5_N, 2).transpose(0, 1, 2, 4, 3).reshape(DEPTH, ng, G, 2 * S5_N)
    fc = state_ffn_conv.reshape(DEPTH, ng, G, FFN_CONV - 1, 2 * D_FF).transpose(0, 1, 3, 2, 4).reshape(
        DEPTH, ng, (FFN_CONV - 1) * G, 2 * D_FF)
    return ret, lh, lc, s5, fc


def kernel(x_prompt, x_sample, state_ret, state_lru_h, state_lru_conv, state_s5, state_ffn_conv, norm1_w, w_in, mix_norm_w, lru_conv_w, lru_conv_b, lru_gate_w, lru_gate_b, lru_lambda, s5_lambda_re, s5_lambda_im, s5_log_dt, s5_B, s5_C, s5_D, s5_glu_w, s5_glu_b, w_out, norm2_w, ffn_up, ffn_conv_w, ffn_conv_b, ffn_down, final_norm_w):
    p = _prepare_params(norm1_w, w_in, mix_norm_w, lru_conv_w, lru_conv_b, lru_gate_w, lru_gate_b, lru_lambda,
                        s5_lambda_re, s5_lambda_im, s5_log_dt, s5_B, s5_C, s5_D, s5_glu_w, s5_glu_b, w_out, norm2_w,
                        ffn_up, ffn_conv_w, ffn_conv_b, ffn_down)
    fnw = final_norm_w[None]

    bp = x_prompt.shape[0]
    zero_states = _states_to_groups(
        jnp.zeros((DEPTH, bp, RET_HEADS, RET_DIM, RET_DIM), F32), jnp.zeros((DEPTH, bp, D_LRU), F32),
        jnp.zeros((DEPTH, bp, LRU_CONV - 1, D_LRU), F32), jnp.zeros((DEPTH, bp, S5_GROUPS, S5_STATE, 2), F32),
        jnp.zeros((DEPTH, bp, FFN_CONV - 1, 2 * D_FF), F32))
    y_prompt, new_p = _trunk(x_prompt, 0, zero_states, p, fnw)
    sample_states = _states_to_groups(state_ret, state_lru_h, state_lru_conv, state_s5, state_ffn_conv)
    y_sample, new_s = _trunk(x_sample, PAST_LEN, sample_states, p, fnw)
    return (y_prompt, y_sample) + new_p + new_s
```

```python
import functools
import math

import numpy as np
import jax
import jax.numpy as jnp
from jax import lax
from jax.experimental import pallas as pl
from jax.experimental.pallas import tpu as pltpu

D_MODEL = 1024
DEPTH = 2
PAST_LEN = 2048
RET_HEADS = 4
RET_DIM = 128
D_RET = RET_HEADS * RET_DIM
D_LRU = 256
LRU_BLOCKS = 4
LRU_BLOCK = D_LRU // LRU_BLOCKS
LRU_CONV = 4
LRU_C = 8.0
D_S5 = 256
S5_GROUP = 16
S5_GROUPS = D_S5 // S5_GROUP
S5_STATE = 64
S5_N = S5_GROUPS * S5_STATE
D_FF = 2816
FFN_CONV = 3
ROPE_BASE = 10000.0
EPS = 1e-6
D_IN = 4 * D_RET + 2 * D_LRU + D_S5

G = 8
LANES = 128
L = 64
R = L * G
MXU_TILE = 256
L_FFN = 128
FF_TILES = (4 * MXU_TILE, 4 * MXU_TILE, 3 * MXU_TILE)
assert sum(FF_TILES) == D_FF
VMEM_LIMIT = 56 * 1024 * 1024

F32 = jnp.float32
BF16 = jnp.bfloat16

_LOG_G = [math.log1p(-(2.0 ** (-5.0 - h))) for h in range(RET_HEADS)]


def _ret_tables():
    t = (np.arange(R) // G).astype(np.float64)
    b = np.arange(R) % G
    lg = np.asarray(_LOG_G)[:, None]
    qdec = np.exp(lg * (t + 1.0))[:, :, None] * np.ones((1, 1, RET_DIM))
    kdec = np.exp(lg * (L - 1.0 - t))[:, :, None] * np.ones((1, 1, RET_DIM)) * RET_DIM ** -0.5
    causal = (b[:, None] == b[None, :]) & (t[:, None] >= t[None, :])
    return causal.astype(np.float32), qdec.astype(np.float32), kdec.astype(np.float32)


_CAUSAL, _QDEC, _KDEC = _ret_tables()
_SDEC = [math.exp(lg * L) for lg in _LOG_G]
_SINV = [math.exp(-lg * L) for lg in _LOG_G]


def _rms(x, w):
    return x * lax.rsqrt(jnp.mean(x * x, axis=-1, keepdims=True) + EPS) * w


def _dot(a, b):
    return jnp.dot(a, b, preferred_element_type=F32)


def _rope_kernel(cos_ref, sin_ref, *, pos0, rows):
    base = pl.program_id(0) * rows + pos0
    lane = lax.broadcasted_iota(jnp.int32, (rows, LANES), 1)
    row = lax.broadcasted_iota(jnp.int32, (rows, LANES), 0)
    half = RET_DIM // 2
    idx = jnp.where(lane >= half, lane - half, lane).astype(F32)
    freq = jnp.exp(idx * (-math.log(ROPE_BASE) / half))
    ang = (row + base).astype(F32) * freq
    s = jnp.sin(ang)
    cos_ref[...] = jnp.cos(ang)
    sin_ref[...] = jnp.where(lane >= half, s, -s)


def _rope_tables(T, pos0):
    rows = min(T, 256)
    return pl.pallas_call(
        functools.partial(_rope_kernel, pos0=pos0, rows=rows),
        grid=(T // rows,),
        out_specs=[pl.BlockSpec((rows, LANES), lambda i: (i, 0))] * 2,
        out_shape=[jax.ShapeDtypeStruct((T, LANES), F32)] * 2,
        name="rope_tables",
    )()


def _s5_prep_kernel(lre_ref, lim_ref, ldt_ref, bre_ref, bim_ref, ab_ref, bbre_ref, bbim_ref):
    dt = jnp.exp(ldt_ref[...])
    lre = lre_ref[...]
    lim = lim_ref[...]
    mag = jnp.exp(lre * dt)
    ph = lim * dt
    abr = mag * jnp.cos(ph)
    abi = mag * jnp.sin(ph)
    den = lre * lre + lim * lim
    nr = abr - 1.0
    fre = (nr * lre + abi * lim) / den
    fim = (abi * lre - nr * lim) / den
    ab_ref[0:1, :] = abr
    ab_ref[1:2, :] = abi
    bre = bre_ref[...]
    bim = bim_ref[...]
    bbre_ref[...] = fre * bre - fim * bim
    bbim_ref[...] = fre * bim + fim * bre


def _s5_discretize(lam_re, lam_im, log_dt, s5_B):
    lre = lam_re.reshape(DEPTH, 1, S5_N)
    lim = lam_im.reshape(DEPTH, 1, S5_N)
    ldt = jnp.broadcast_to(log_dt[:, :, None], (DEPTH, S5_GROUPS, S5_STATE)).reshape(DEPTH, 1, S5_N)
    bt = s5_B.transpose(0, 3, 1, 2, 4).reshape(DEPTH, S5_GROUP, S5_N, 2)
    row = lambda r: pl.BlockSpec((None, r, S5_N), lambda d: (d, 0, 0))
    return pl.pallas_call(
        _s5_prep_kernel,
        grid=(DEPTH,),
        in_specs=[row(1), row(1), row(1), row(S5_GROUP), row(S5_GROUP)],
        out_specs=[row(2), row(S5_GROUP), row(S5_GROUP)],
        out_shape=[jax.ShapeDtypeStruct((DEPTH, 2, S5_N), F32),
                   jax.ShapeDtypeStruct((DEPTH, S5_GROUP, S5_N), F32),
                   jax.ShapeDtypeStruct((DEPTH, S5_GROUP, S5_N), F32)],
        name="s5_discretize",
    )(lre, lim, ldt, bt[..., 0], bt[..., 1])


def _s5_in_matrix(bb):
    gi = jnp.arange(S5_GROUPS)
    same = gi[:, None, None, None] == gi[None, None, :, None]
    full = jnp.where(same, bb.reshape(DEPTH, 1, S5_GROUP, S5_GROUPS, S5_STATE), 0.0)
    return full.reshape(DEPTH, D_S5, S5_N)


def _s5_out_matrix(c):
    gi = jnp.arange(S5_GROUPS)
    same = gi[:, None, None, None] == gi[None, None, :, None]
    full = jnp.where(same, c.transpose(0, 3, 1, 2)[:, None], 0.0)
    return full.reshape(DEPTH, S5_N, D_S5)


def _lru_gate_matrix(gw):
    bi = jnp.arange(LRU_BLOCKS)
    same = bi[:, None, None, None, None] == bi[None, None, None, :, None]
    wt = gw.transpose(0, 2, 3, 1, 4)[:, :, :, :, None, :]
    return jnp.where(same, wt, 0.0).reshape(DEPTH, D_LRU, 2 * D_LRU)


def _time_major(x3_ref, slab_scr):
    for b in range(G):
        for j in range(D_MODEL // LANES):
            slab_scr[j, pl.ds(b, L, stride=G), :] = x3_ref[b, :, j * LANES:(j + 1) * LANES]


def _mixer_kernel(x_ref, cos_ref, sin_ref, sret_ref, slh_ref, slc_ref, ss5_ref,
                  causal_ref, qdec_ref, kdec_ref,
                  n1w_ref, win_ref, mnw_ref, lcw_ref, lcb_ref, wg_ref, gb_ref, lam_ref,
                  ab_ref, bmat_ref, cre_ref, cim_ref, s5d_ref, gluw_ref, glub_ref, wout_ref,
                  xo_ref, ret_ref, lh_ref, lc_ref, s5s_ref,
                  h_scr, q_scr, k_scr, v_scr, cross_scr, scat_scr, g_scr, ext_scr, lgate_scr, a_scr, b_scr,
                  u_scr, bu_scr, y_scr, mix_scr, *maybe_xt_scr, first_layer):
    c = pl.program_id(1)
    n_carry = (LRU_CONV - 1) * G

    @pl.when(c == 0)
    def _():
        for b in range(G):
            for hh in range(RET_HEADS):
                scat_scr[hh, :, b * RET_DIM:(b + 1) * RET_DIM] = sret_ref[b, hh]
        lh_ref[...] = slh_ref[...]
        ext_scr[0:n_carry, :] = slc_ref[...]
        s5s_ref[...] = ss5_ref[...]

    if first_layer:
        xt_scr, = maybe_xt_scr
        _time_major(x_ref, xt_scr)

        def x_rows():
            return jnp.concatenate([xt_scr[j] for j in range(D_MODEL // LANES)], axis=1)
    else:
        def x_rows():
            return x_ref[...]

    mnw = mnw_ref[...]
    h_scr[...] = _rms(x_rows(), n1w_ref[...]).astype(BF16)

    def project(col0, width):
        return _dot(h_scr[...], win_ref[:, col0:col0 + width])

    def project_heads(j, dst):
        res = project(j * D_RET, D_RET)
        for hh in range(RET_HEADS):
            dst[hh] = res[:, hh * RET_DIM:(hh + 1) * RET_DIM]

    res = project(4 * D_RET, 2 * D_LRU)
    ext_scr[n_carry:n_carry + R, :] = res[:, :D_LRU]
    lgate_scr[...] = res[:, D_LRU:]
    u_scr[...] = project(4 * D_RET + 2 * D_LRU, D_S5)
    project_heads(0, q_scr)
    bu_scr[...] = _dot(u_scr[...].astype(BF16), bmat_ref[...])

    conv = lcb_ref[...] + lcw_ref[0:1, :] * ext_scr[0:R, :]
    for j in range(1, LRU_CONV):
        conv = conv + lcw_ref[j:j + 1, :] * ext_scr[j * G:j * G + R, :]
    carry_rows = ext_scr[R:R + n_carry, :]
    ext_scr[0:n_carry, :] = carry_rows
    lc_ref[...] = carry_rows
    gates = _dot(conv.astype(BF16), wg_ref[...]) + gb_ref[...]
    z = -lam_ref[...]
    softplus = jnp.maximum(z, 0.0) + jnp.log1p(jnp.exp(-jnp.abs(z)))
    log_a = -LRU_C * jax.nn.sigmoid(gates[:, :D_LRU]) * softplus
    a = jnp.exp(log_a)
    a_scr[...] = a
    b_scr[...] = jnp.sqrt(-jnp.tanh(log_a) * (a * a + 1.0)) * (jax.nn.sigmoid(gates[:, D_LRU:]) * conv)

    project_heads(1, k_scr)

    h_t = lh_ref[...]
    for t in range(L):
        rows = slice(t * G, (t + 1) * G)
        h_t = a_scr[rows, :] * h_t + b_scr[rows, :]
        b_scr[rows, :] = h_t
    lh_ref[...] = h_t
    lru_out = _rms(jax.nn.gelu(lgate_scr[...]) * b_scr[...], mnw[:, D_RET:D_RET + D_LRU])
    mix_scr[:, D_RET:D_RET + D_LRU] = lru_out.astype(BF16)

    project_heads(2, v_scr)

    ar = jnp.broadcast_to(ab_ref[0:1, :], (G, S5_N))
    ai = jnp.broadcast_to(ab_ref[1:2, :], (G, S5_N))
    xr = s5s_ref[:, 0:S5_N]
    xi = s5s_ref[:, S5_N:2 * S5_N]
    for t in range(L):
        rows = slice(t * G, (t + 1) * G)
        xr, xi = (ar * xr - ai * xi + bu_scr[rows, 0:S5_N],
                  ar * xi + ai * xr + bu_scr[rows, S5_N:2 * S5_N])
        bu_scr[rows, 0:S5_N] = xr
        bu_scr[rows, S5_N:2 * S5_N] = xi
    s5s_ref[:, 0:S5_N] = xr
    s5s_ref[:, S5_N:2 * S5_N] = xi

    g_scr[...] = project(3 * D_RET, D_RET)

    t0 = pl.multiple_of(c * L, L)
    cosv = jnp.repeat(cos_ref[pl.ds(t0, L), :], G, axis=0)
    sinv = jnp.repeat(sin_ref[pl.ds(t0, L), :], G, axis=0)

    def retention_head(hh):
        cols = slice(hh * RET_DIM, (hh + 1) * RET_DIM)
        q = q_scr[hh]
        k = k_scr[hh]
        v = v_scr[hh]
        qs_f32 = (q * cosv + pltpu.roll(q, RET_DIM // 2, 1) * sinv) * qdec_ref[hh]
        qs = qs_f32.astype(BF16)
        ks_f32 = (k * cosv + pltpu.roll(k, RET_DIM // 2, 1) * sinv) * kdec_ref[hh]
        ks = ks_f32.astype(BF16)
        vb = v.astype(BF16)
        inner = []
        for r0, n_keys in ((0, R // 2), (R // 2, R)):
            scores = lax.dot_general(qs[r0:r0 + R // 2], ks[0:n_keys], (((1,), (1,)), ((), ())),
                                     preferred_element_type=F32) * causal_ref[r0:r0 + R // 2, 0:n_keys]
            inner.append(_dot(scores.astype(BF16), vb[0:n_keys]))
        q_scr[hh] = qs_f32
        k_scr[hh] = ks_f32
        for b in range(G):
            rows = pl.ds(b, L, stride=G)
            lanes = slice(b * RET_DIM, (b + 1) * RET_DIM)
            s_old = scat_scr[hh, :, lanes]
            cross_scr[hh, rows, :] = _dot(q_scr[hh, rows, :].astype(BF16), s_old.astype(BF16))
            scat_scr[hh, :, lanes] = _SDEC[hh] * s_old + lax.dot_general(
                k_scr[hh, rows, :].astype(BF16), v_scr[hh, rows, :].astype(BF16),
                (((0,), (0,)), ((), ())), preferred_element_type=F32)
        o = jnp.concatenate(inner, axis=0) * _SINV[hh] + cross_scr[hh]
        o = o * lax.rsqrt(jnp.mean(o * o, axis=-1, keepdims=True) + EPS)
        mix_scr[:, cols] = (o * mnw[:, cols] * jax.nn.silu(g_scr[:, cols])).astype(BF16)

    retention_head(0)
    y_scr[...] = (_dot(bu_scr[:, 0:S5_N].astype(BF16), cre_ref[...])
                  - _dot(bu_scr[:, S5_N:2 * S5_N].astype(BF16), cim_ref[...]) + s5d_ref[...] * u_scr[...])
    retention_head(1)
    zz = jax.nn.gelu(y_scr[...])
    glu = zz * jax.nn.sigmoid(_dot(zz.astype(BF16), gluw_ref[...]) + glub_ref[...])
    y_scr[...] = glu
    mix_scr[:, D_RET + D_LRU:] = _rms(y_scr[...], mnw[:, D_RET + D_LRU:]).astype(BF16)

    def out_project(col0, width):
        return _dot(mix_scr[:, col0:col0 + width], wout_ref[col0:col0 + width, :])

    xo_ref[...] = x_rows() + out_project(D_RET, D_LRU + D_S5)
    xo_ref[...] += out_project(0, D_RET // 2)
    retention_head(2)
    retention_head(3)

    @pl.when(c == pl.num_programs(1) - 1)
    def _():
        for b in range(G):
            for hh in range(RET_HEADS):
                ret_ref[b, hh] = scat_scr[hh, :, b * RET_DIM:(b + 1) * RET_DIM]

    xo_ref[...] += out_project(D_RET // 2, D_RET // 2)


def _const_spec(shape):
    nd = len(shape)
    return pl.BlockSpec(shape, lambda g, c: (0,) * nd)


def _layer_spec(shape, layer):
    nd = len(shape) - 1
    return pl.BlockSpec((None,) + tuple(shape[1:]), lambda g, c: (layer,) + (0,) * nd)


def _state_spec(shape, layer=None):
    nd = len(shape)
    if layer is None:
        return pl.BlockSpec((None,) + shape, lambda g, c: (g,) + (0,) * nd)
    return pl.BlockSpec((None, None) + shape, lambda g, c: (layer, g) + (0,) * nd)


_MIXER_WEIGHTS = ("n1w", "w_in", "mnw", "lcw", "lcb", "wg", "gb", "lam", "ab", "bmat", "cre", "cim", "s5d",
                  "gluw", "glub", "w_out")
_FFN_WEIGHTS = ("n2w", "ffn_up", "fcw", "fcb", "ffn_down")


def _mixer(x, cos2, sin2, st_ret, st_lh, st_lc, st_s5, p, layer):
    first_layer = layer == 0
    if first_layer:
        ng, nc = x.shape[0] // G, x.shape[1] // L
        in_block = (G, L, D_MODEL)
    else:
        ng, nc = x.shape[0], x.shape[1] // R
        in_block = (None, R, D_MODEL)

    x_spec = pl.BlockSpec((None, R, D_MODEL), lambda g, c: (g, c, 0))
    t_spec = _const_spec(cos2.shape)
    consts = (_CAUSAL, _QDEC, _KDEC)
    weights = tuple(p[n] for n in _MIXER_WEIGHTS)
    state_shapes = ((G, RET_HEADS, RET_DIM, RET_DIM), (G, D_LRU), ((LRU_CONV - 1) * G, D_LRU), (G, 2 * S5_N))
    return pl.pallas_call(
        functools.partial(_mixer_kernel, first_layer=first_layer),
        grid=(ng, nc),
        in_specs=[pl.BlockSpec(in_block, lambda g, c: (g, c, 0)), t_spec, t_spec]
                 + [_state_spec(s, layer) for s in state_shapes]
                 + [_const_spec(a.shape) for a in consts] + [_layer_spec(w.shape, layer) for w in weights],
        out_specs=[x_spec] + [_state_spec(s) for s in state_shapes],
        out_shape=[jax.ShapeDtypeStruct((ng, nc * R, D_MODEL), F32)]
                  + [jax.ShapeDtypeStruct((ng,) + s, F32) for s in state_shapes],
        scratch_shapes=[
            pltpu.VMEM((R, D_MODEL), BF16),
            pltpu.VMEM((RET_HEADS, R, RET_DIM), F32),
            pltpu.VMEM((RET_HEADS, R, RET_DIM), F32),
            pltpu.VMEM((RET_HEADS, R, RET_DIM), F32),
            pltpu.VMEM((RET_HEADS, R, RET_DIM), F32),
            pltpu.VMEM((RET_HEADS, RET_DIM, G * RET_DIM), F32),
            pltpu.VMEM((R, D_RET), F32),
            pltpu.VMEM((R + (LRU_CONV - 1) * G, D_LRU), F32),
            pltpu.VMEM((R, D_LRU), F32),
            pltpu.VMEM((R, D_LRU), F32),
            pltpu.VMEM((R, D_LRU), F32),
            pltpu.VMEM((R, D_S5), F32),
            pltpu.VMEM((R, 2 * S5_N), F32),
            pltpu.VMEM((R, D_S5), F32),
            pltpu.VMEM((R, D_MODEL), BF16),
        ] + ([pltpu.VMEM((D_MODEL // LANES, R, LANES), F32)] if first_layer else []),
        compiler_params=pltpu.CompilerParams(
            dimension_semantics=("arbitrary", "arbitrary"), vmem_limit_bytes=VMEM_LIMIT),
        name="mixer",
    )(x, cos2, sin2, st_ret, st_lh, st_lc, st_s5, *consts, *weights)


def _ffn_kernel(x_ref, sfc_ref, n2w_ref, up_ref, cw_ref, cb_ref, down_ref, fnw_ref,
                xo_ref, fc_ref, h_scr, extg_scr, extv_scr, *maybe_y_scr, last_layer):
    c = pl.program_id(1)
    n_carry = (FFN_CONV - 1) * G
    rows = x_ref.shape[0]

    @pl.when(c == 0)
    def _():
        fc_ref[...] = sfc_ref[...]

    h_scr[...] = _rms(x_ref[...], n2w_ref[...]).astype(BF16)
    acc = x_ref[...]
    col0 = 0
    for width in FF_TILES:
        halves = [(ext, slice(half * D_FF + col0, half * D_FF + col0 + width))
                  for half, ext in enumerate((extg_scr, extv_scr))]
        for ext, cols in halves:
            ext[0:n_carry, 0:width] = fc_ref[:, cols]
            ext[n_carry:n_carry + rows, 0:width] = _dot(h_scr[...], up_ref[:, cols])
            fc_ref[:, cols] = ext[rows:rows + n_carry, 0:width]
        acts = []
        for r0 in range(0, rows, rows // 2):
            conv = []
            for ext, cols in halves:
                y = cb_ref[:, cols] + cw_ref[0:1, cols] * ext[r0:r0 + rows // 2, 0:width]
                for k in range(1, FFN_CONV):
                    y = y + cw_ref[k:k + 1, cols] * ext[r0 + k * G:r0 + k * G + rows // 2, 0:width]
                conv.append(y)
            acts.append((jax.nn.silu(conv[0]) * conv[1]).astype(BF16))
        act = jnp.concatenate(acts, axis=0)
        acc = acc + _dot(act, down_ref[col0:col0 + width, :])
        col0 += width
    if not last_layer:
        xo_ref[...] = acc
        return
    y_scr, = maybe_y_scr
    acc = _rms(acc, fnw_ref[...])
    for j in range(D_MODEL // LANES):
        y_scr[j] = acc[:, j * LANES:(j + 1) * LANES]
    steps = rows // G
    for b in range(G):
        for j in range(D_MODEL // LANES):
            xo_ref[b, :, j * LANES:(j + 1) * LANES] = y_scr[j, pl.ds(b, steps, stride=G), :]


def _ffn(xg, st_fc, p, fnw, layer, batch):
    ng, total_rows, _ = xg.shape
    steps = min(L_FFN, total_rows // G)
    rows = steps * G
    nc = total_rows // rows
    last_layer = layer == DEPTH - 1
    x_spec = pl.BlockSpec((None, rows, D_MODEL), lambda g, c: (g, c, 0))
    fc_shape = ((FFN_CONV - 1) * G, 2 * D_FF)
    weights = tuple(p[n] for n in _FFN_WEIGHTS)
    scratch = [
        pltpu.VMEM((rows, D_MODEL), BF16),
        pltpu.VMEM((rows + (FFN_CONV - 1) * G, max(FF_TILES)), F32),
        pltpu.VMEM((rows + (FFN_CONV - 1) * G, max(FF_TILES)), F32),
    ]
    if last_layer:
        out_x_spec = pl.BlockSpec((G, steps, D_MODEL), lambda g, c: (g, c, 0))
        out_x_shape = jax.ShapeDtypeStruct((batch, total_rows // G, D_MODEL), F32)
        scratch.append(pltpu.VMEM((D_MODEL // LANES, rows, LANES), F32))
    else:
        out_x_spec, out_x_shape = x_spec, jax.ShapeDtypeStruct(xg.shape, F32)
    return pl.pallas_call(
        functools.partial(_ffn_kernel, last_layer=last_layer),
        grid=(ng, nc),
        in_specs=[x_spec, _state_spec(fc_shape, layer)]
                 + [_layer_spec(w.shape, layer) for w in weights] + [_const_spec(fnw.shape)],
        out_specs=[out_x_spec, _state_spec(fc_shape)],
        out_shape=[out_x_shape, jax.ShapeDtypeStruct((ng,) + fc_shape, F32)],
        scratch_shapes=scratch,
        compiler_params=pltpu.CompilerParams(
            dimension_semantics=("arbitrary", "arbitrary"), vmem_limit_bytes=VMEM_LIMIT),
        name="convffn",
    )(xg, st_fc, *weights, fnw)


def _prepare_params(norm1_w, w_in, mix_norm_w, lru_conv_w, lru_conv_b, lru_gate_w, lru_gate_b, lru_lambda,
                    s5_lambda_re, s5_lambda_im, s5_log_dt, s5_B, s5_C, s5_D, s5_glu_w, s5_glu_b, w_out, norm2_w,
                    ffn_up, ffn_conv_w, ffn_conv_b, ffn_down):
    s5_ab, s5_bbre, s5_bbim = _s5_discretize(s5_lambda_re, s5_lambda_im, s5_log_dt, s5_B)
    vec = lambda a: a.reshape(DEPTH, 1, -1)
    return {
        "n1w": vec(norm1_w), "w_in": w_in.astype(BF16), "mnw": vec(mix_norm_w),
        "lcw": lru_conv_w, "lcb": vec(lru_conv_b),
        "wg": _lru_gate_matrix(lru_gate_w).astype(BF16), "gb": vec(lru_gate_b),
        "lam": vec(lru_lambda), "ab": s5_ab,
        "bmat": jnp.concatenate([_s5_in_matrix(s5_bbre), _s5_in_matrix(s5_bbim)], axis=2).astype(BF16),
        "cre": _s5_out_matrix(s5_C[..., 0]).astype(BF16), "cim": _s5_out_matrix(s5_C[..., 1]).astype(BF16),
        "s5d": vec(s5_D), "gluw": s5_glu_w.astype(BF16), "glub": vec(s5_glu_b),
        "w_out": w_out.astype(BF16), "n2w": vec(norm2_w), "ffn_up": ffn_up.astype(BF16),
        "fcw": ffn_conv_w, "fcb": vec(ffn_conv_b), "ffn_down": ffn_down.astype(BF16),
    }


def _trunk(x, pos0, states, p, fnw):
    b, t, _ = x.shape
    ng = b // G
    xg = x
    assert b % G == 0 and t % L == 0 and (t % L_FFN == 0 or t < L_FFN), (b, t)
    cos2, sin2 = _rope_tables(t, pos0)
    st_ret, st_lh, st_lc, st_s5, st_fc = states
    new = []
    for layer in range(DEPTH):
        xg, ret, lh, lc, s5 = _mixer(xg, cos2, sin2, st_ret, st_lh, st_lc, st_s5, p, layer)
        xg, fc = _ffn(xg, st_fc, p, fnw, layer, b)
        new.append((ret, lh, lc, s5, fc))
    ret, lh, lc, s5, fc = (jnp.stack([n[i] for n in new]) for i in range(5))
    ret = ret.reshape(DEPTH, b, RET_HEADS, RET_DIM, RET_DIM)
    lh = lh.reshape(DEPTH, b, D_LRU)
    lc = lc.reshape(DEPTH, ng, LRU_CONV - 1, G, D_LRU).transpose(0, 1, 3, 2, 4).reshape(DEPTH, b, LRU_CONV - 1, D_LRU)
    s5 = s5.reshape(DEPTH, b, 2, S5_GROUPS, S5_STATE).transpose(0, 1, 3, 4, 2)
    fc = fc.reshape(DEPTH, ng, FFN_CONV - 1, G, 2 * D_FF).transpose(0, 1, 3, 2, 4).reshape(DEPTH, b, FFN_CONV - 1, 2 * D_FF)
    return xg, (ret, lh, lc, s5, fc)


def _states_to_groups(state_ret, state_lru_h, state_lru_conv, state_s5, state_ffn_conv):
    b = state_ret.shape[1]
    ng = b // G
    ret = state_ret.reshape(DEPTH, ng, G, RET_HEADS, RET_DIM, RET_DIM)
    lh = state_lru_h.reshape(DEPTH, ng, G, D_LRU)
    lc = state_lru_conv.reshape(DEPTH, ng, G, LRU_CONV - 1, D_LRU).transpose(0, 1, 3, 2, 4).reshape(
        DEPTH, ng, (LRU_CONV - 1) * G, D_LRU)
    s5 = state_s5.reshape(DEPTH, ng, G, S5_N, 2).transpose(0, 1, 2, 4, 3).reshape(DEPTH, ng, G, 2 * S5_N)
    fc = state_ffn_conv.reshape(DEPTH, ng, G, FFN_CONV - 1, 2 * D_FF).transpose(0, 1, 3, 2, 4).reshape(
        DEPTH, ng, (FFN_CONV - 1) * G, 2 * D_FF)
    return ret, lh, lc, s5, fc


def kernel(x_prompt, x_sample, state_ret, state_lru_h, state_lru_conv, state_s5, state_ffn_conv, norm1_w, w_in, mix_norm_w, lru_conv_w, lru_conv_b, lru_gate_w, lru_gate_b, lru_lambda, s5_lambda_re, s5_lambda_im, s5_log_dt, s5_B, s5_C, s5_D, s5_glu_w, s5_glu_b, w_out, norm2_w, ffn_up, ffn_conv_w, ffn_conv_b, ffn_down, final_norm_w):
    p = _prepare_params(norm1_w, w_in, mix_norm_w, lru_conv_w, lru_conv_b, lru_gate_w, lru_gate_b, lru_lambda,
                        s5_lambda_re, s5_lambda_im, s5_log_dt, s5_B, s5_C, s5_D, s5_glu_w, s5_glu_b, w_out, norm2_w,
                        ffn_up, ffn_conv_w, ffn_conv_b, ffn_down)
    fnw = final_norm_w[None]

    bp = x_prompt.shape[0]
    zero_states = _states_to_groups(
        jnp.zeros((DEPTH, bp, RET_HEADS, RET_DIM, RET_DIM), F32), jnp.zeros((DEPTH, bp, D_LRU), F32),
        jnp.zeros((DEPTH, bp, LRU_CONV - 1, D_LRU), F32), jnp.zeros((DEPTH, bp, S5_GROUPS, S5_STATE, 2), F32),
        jnp.zeros((DEPTH, bp, FFN_CONV - 1, 2 * D_FF), F32))
    y_prompt, new_p = _trunk(x_prompt, 0, zero_states, p, fnw)
    sample_states = _states_to_groups(state_ret, state_lru_h, state_lru_conv, state_s5, state_ffn_conv)
    y_sample, new_s = _trunk(x_sample, PAST_LEN, sample_states, p, fnw)
    return (y_prompt, y_sample) + new_p + new_s
```
